```python
import math
import jax, jax.numpy as jnp
from jax import lax
import numpy as np

D_MODEL = 1024
BATCH = 4
SEQ = 4096
DEPTH = 2
DEC_BATCH = 32
DEC_SEQ = 1
PAST_LEN = 16384
PAGE_SIZE = 128

W_MIX = D_MODEL // 4
N_BRANCH = 4
CONV_W = 31
H_B = 4
DK_B = W_MIX // H_B
DV_B = W_MIX // H_B
HGRN_CHUNK = 64
H_C = 4
DH_C = W_MIX // H_C
H_D = 4
DH_D = W_MIX // (2 * H_D)
D_FF = 4 * D_MODEL
QBLK = 128
EPS = 1e-6
NEG = -1e30
LB_FLOOR = 1e-30
_IN_SIZES = (W_MIX,) * 2 + (W_MIX,) * 4 + (W_MIX,) * 3 + (H_C,) + (W_MIX,) * 3 + (D_MODEL,) * 4
N_IN = sum(_IN_SIZES)

kernel_name = 'hybrid_gated_parallel_decoder_step'


def rmsnorm(x, g):
    xf = x.astype(jnp.float32)
    y = xf * lax.rsqrt(jnp.mean(xf * xf, axis=-1, keepdims=True) + EPS)
    return (y * g.astype(jnp.float32)).astype(x.dtype)


def layernorm(x, g, b):
    xf = x.astype(jnp.float32)
    mu = jnp.mean(xf, axis=-1, keepdims=True)
    var = jnp.mean(jnp.square(xf - mu), axis=-1, keepdims=True)
    return ((xf - mu) * lax.rsqrt(var + EPS) * g.astype(jnp.float32) + b.astype(jnp.float32)).astype(x.dtype)


def alibi_slopes(n):
    return jnp.asarray(2.0 ** (-8.0 * np.arange(1, n + 1) / n), jnp.float32)


def causal_dwconv(xp, w, b):
    y = lax.conv_general_dilated(xp, w.astype(xp.dtype)[:, None, :], window_strides=(1,), padding='VALID',
                                 dimension_numbers=('NWC', 'WIO', 'NWC'), feature_group_count=W_MIX)
    return y + b.astype(xp.dtype)


def hgrn2_recurrence(q, k, v, logf, S0):
    Bt, L = q.shape[:2]
    C = min(HGRN_CHUNK, L)
    n = -(-L // C)
    pad = n * C - L

    def prep(a):
        a = jnp.pad(a.astype(jnp.float32), ((0, 0), (0, pad), (0, 0), (0, 0)))
        return a.reshape(Bt, n, C, a.shape[2], a.shape[3]).transpose(1, 0, 3, 2, 4)

    qs, ks, vs, gs = prep(q), prep(k), prep(v), prep(logf)
    causal = jnp.tril(jnp.ones((C, C), bool))

    def step(S, inp):
        qc, kc, vc, gc = inp
        G = jnp.cumsum(gc, axis=2)
        diff = jnp.where(causal[:, :, None], G[:, :, :, None, :] - G[:, :, None, :, :], NEG)
        A = jnp.einsum('bhtd,bhsd,bhtsd->bhts', qc, kc, jnp.exp(diff))
        o = jnp.einsum('bhtd,bhde->bhte', qc * jnp.exp(G), S) + jnp.einsum('bhts,bhse->bhte', A, vc)
        Gl = G[:, :, -1:, :]
        S = jnp.exp(Gl[:, :, 0, :])[..., None] * S + jnp.einsum('bhsd,bhse->bhde', kc * jnp.exp(Gl - G), vc)
        return S, o

    S, o = lax.scan(step, S0.astype(jnp.float32), (qs, ks, vs, gs))
    o = o.transpose(1, 0, 3, 2, 4).reshape(Bt, n * C, q.shape[2], v.shape[3])[:, :L]
    return o, S


def fox_prompt(q, k, v, logf):
    Bt, S = q.shape[:2]
    nb = S // QBLK
    scale = DH_C ** -0.5
    cum = jnp.cumsum(logf.astype(jnp.float32), axis=1).transpose(0, 2, 1)
    qb = q.reshape(Bt, nb, QBLK, H_C, DH_C).swapaxes(0, 1)
    cb = cum.reshape(Bt, H_C, nb, QBLK).transpose(2, 0, 1, 3)
    kpos = jnp.arange(S)

    def blk(args):
        i, qi, ci = args
        s = jnp.einsum('bqhd,bkhd->bhqk', qi, k).astype(jnp.float32) * scale + ci[..., None] - cum[:, :, None, :]
        qpos = i * QBLK + jnp.arange(QBLK)
        s = jnp.where(qpos[:, None] >= kpos[None, :], s, NEG)
        p = jax.nn.softmax(s, axis=-1)
        return jnp.einsum('bhqk,bkhd->bqhd', p.astype(v.dtype), v)

    o = lax.map(blk, (jnp.arange(nb), qb, cb))
    return o.swapaxes(0, 1).reshape(Bt, S, H_C, DH_C)


def fox_sample(q, k, v, logf, kp, vp, lfp):
    T = q.shape[1]
    P = kp.shape[1]
    scale = DH_C ** -0.5
    cnew = jnp.cumsum(logf.astype(jnp.float32), axis=1).transpose(0, 2, 1)
    lfp = lfp.astype(jnp.float32)
    suf = (lax.cumsum(lfp, axis=1, reverse=True) - lfp).transpose(0, 2, 1)
    s_past = jnp.einsum('bthd,bshd->bhts', q, kp).astype(jnp.float32) * scale + cnew[..., None] + suf[:, :, None, :]
    s_new = jnp.einsum('bthd,bshd->bhts', q, k).astype(jnp.float32) * scale + cnew[..., :, None] - cnew[..., None, :]
    s_new = jnp.where(jnp.tril(jnp.ones((T, T), bool)), s_new, NEG)
    p = jax.nn.softmax(jnp.concatenate([s_past, s_new], axis=-1), axis=-1).astype(v.dtype)
    return jnp.einsum('bhts,bshd->bthd', p[..., :P], vp) + jnp.einsum('bhts,bshd->bthd', p[..., P:], v)


def diff_prompt(q, k, v, lam):
    Bt, S = q.shape[:2]
    nb = S // QBLK
    scale = DH_D ** -0.5
    slopes = alibi_slopes(H_D)[None, :, None, None, None]
    qb = q.reshape(Bt, nb, QBLK, H_D, 2, DH_D).swapaxes(0, 1)
    kpos = jnp.arange(S)

    def blk(args):
        i, qi = args
        qpos = i * QBLK + jnp.arange(QBLK)
        dist = qpos[:, None] - kpos[None, :]
        s = jnp.einsum('bqhmd,bkhmd->bhmqk', qi, k).astype(jnp.float32) * scale - slopes * dist.astype(jnp.float32)
        s = jnp.where(dist >= 0, s, NEG)
        p = jax.nn.softmax(s, axis=-1)
        a = (p[:, :, 0] - lam * p[:, :, 1]).astype(v.dtype)
        return jnp.einsum('bhqk,bkhe->bqhe', a, v)

    o = lax.map(blk, (jnp.arange(nb), qb))
    return o.swapaxes(0, 1).reshape(Bt, S, H_D, 2 * DH_D)


def diff_sample(q, k, v, kp, vp, lam):
    T = q.shape[1]
    P = kp.shape[1]
    scale = DH_D ** -0.5
    slopes = alibi_slopes(H_D)[None, :, None, None, None]
    d_past = ((P + jnp.arange(T))[:, None] - jnp.arange(P)[None, :]).astype(jnp.float32)
    d_new = jnp.arange(T)[:, None] - jnp.arange(T)[None, :]
    s_past = jnp.einsum('bthmd,bshmd->bhmts', q, kp).astype(jnp.float32) * scale - slopes * d_past
    s_new = jnp.einsum('bthmd,bshmd->bhmts', q, k).astype(jnp.float32) * scale - slopes * d_new.astype(jnp.float32)
    s_new = jnp.where(d_new >= 0, s_new, NEG)
    p = jax.nn.softmax(jnp.concatenate([s_past, s_new], axis=-1), axis=-1)
    a = (p[:, :, 0] - lam * p[:, :, 1]).astype(v.dtype)
    return jnp.einsum('bhts,bshe->bthe', a[..., :P], vp) + jnp.einsum('bhts,bshe->bthe', a[..., P:], v)


def _project(h, w_in_l, b_fox_l, lb_l, fqn, fkn, dqn, dkn):
    Bt, L, _ = h.shape
    z = jnp.einsum('bld,dn->bln', h, w_in_l)
    idx = np.cumsum(_IN_SIZES)[:-1].tolist()
    (a_val, a_gate, b_q, b_f, b_i, b_g, c_q, c_k, c_v, c_f, d_q, d_k, d_v,
     g_a, g_b, g_c, g_d) = jnp.split(z, idx, axis=-1)
    glu = a_val * jax.nn.sigmoid(a_gate)
    lb = lb_l.astype(jnp.float32)
    b_logf = jnp.logaddexp(jnp.log1p(-lb) + jax.nn.log_sigmoid(b_f.astype(jnp.float32)),
                           jnp.log(jnp.maximum(lb, LB_FLOOR)))
    b_logf = b_logf.reshape(Bt, L, H_B, DK_B)
    b_k = -jnp.expm1(b_logf)
    b_q = jax.nn.silu(b_q).reshape(Bt, L, H_B, DK_B)
    b_v = b_i.reshape(Bt, L, H_B, DV_B)
    c_q = rmsnorm(c_q.reshape(Bt, L, H_C, DH_C), fqn)
    c_k = rmsnorm(c_k.reshape(Bt, L, H_C, DH_C), fkn)
    c_v = c_v.reshape(Bt, L, H_C, DH_C)
    c_logf = jax.nn.log_sigmoid(c_f.astype(jnp.float32) + b_fox_l.astype(jnp.float32))
    d_q = rmsnorm(d_q.reshape(Bt, L, H_D, 2, DH_D), dqn)
    d_k = rmsnorm(d_k.reshape(Bt, L, H_D, 2, DH_D), dkn)
    d_v = d_v.reshape(Bt, L, H_D, 2 * DH_D)
    return glu, (b_q, b_k, b_v, b_logf), b_g, (c_q, c_k, c_v, c_logf), (d_q, d_k, d_v), (g_a, g_b, g_c, g_d)


def _trunk(x, c, mix, P):
    lbs = jax.nn.softmax(P['lb_logits'].astype(jnp.float32), axis=0)
    lbs = jnp.cumsum(lbs, axis=0) - lbs[0]
    states = []
    for l in range(DEPTH):
        Bt, L, _ = x.shape
        mod = jnp.einsum('bd,dn->bn', jax.nn.silu(c), P['w_ada'][l]) + P['b_ada'][l]
        sh1, sc1, gt1, sh2, sc2, gt2 = [m[:, None, :] for m in jnp.split(mod, 6, axis=-1)]
        h = rmsnorm(x, P['norm1_g'][l]) * (1 + sc1) + sh1
        glu, fb, b_g, fc, fd, gates = _project(h, P['w_in'][l], P['b_fox_f'][l], lbs[l], P['fox_qn_g'][l],
                                                P['fox_kn_g'][l], P['diff_qn_g'][l], P['diff_kn_g'][l])
        lam_init = 0.8 - 0.6 * math.exp(-0.3 * l)
        lam = (jnp.exp(jnp.sum(P['lam_q1'][l] * P['lam_k1'][l])) - jnp.exp(jnp.sum(P['lam_q2'][l] * P['lam_k2'][l]))
               + lam_init).astype(jnp.float32)
        conv, ob, oc, od, st = mix(l, glu, fb, fc, fd, lam)
        ya = jax.nn.silu(layernorm(conv, P['conv_ln_g'][l], P['conv_ln_b'][l]))
        yb = rmsnorm(ob, P['hgrn_norm_g'][l]).reshape(Bt, L, W_MIX).astype(x.dtype) * jax.nn.silu(b_g)
        yc = oc.reshape(Bt, L, W_MIX).astype(x.dtype)
        yd = (rmsnorm(od, P['diff_norm_g'][l]) * (1 - lam_init)).reshape(Bt, L, W_MIX).astype(x.dtype)
        merged = 0.0
        for j, (y, g) in enumerate(zip((ya, yb, yc, yd), gates)):
            merged = merged + jax.nn.sigmoid(g) * jnp.einsum('blw,wd->bld', y, P['w_branch'][l, j])
        x = x + gt1 * jnp.einsum('bld,de->ble', merged, P['w_out'][l])
        h2 = rmsnorm(x, P['norm2_g'][l]) * (1 + sc2) + sh2
        ff = jnp.square(jax.nn.relu(jnp.einsum('bld,df->blf', h2, P['w_ff1'][l])))
        x = x + gt2 * jnp.einsum('blf,fd->bld', ff, P['w_ff2'][l])
        states.append(st)
    stacked = [jnp.stack([st[i] for st in states]) for i in range(7)]
    return x, stacked


def setup_inputs(seed: int = 0) -> dict:
    key = jax.random.key(seed)
    it = iter(list(jax.random.split(key, 48)))

    def nrm(shape, scale=1.0):
        return jax.random.normal(next(it), shape, jnp.float32) * scale

    n_pages = PAST_LEN // PAGE_SIZE
    n_used = DEC_BATCH * n_pages
    n_pool = n_used + n_used // 4
    d = {}
    d['x_prompt'] = nrm((BATCH, SEQ, D_MODEL))
    d['x_sample'] = nrm((DEC_BATCH, DEC_SEQ, D_MODEL))
    d['cache_fox_k'] = nrm((DEPTH, n_pool, PAGE_SIZE, H_C, DH_C))
    d['cache_fox_v'] = nrm((DEPTH, n_pool, PAGE_SIZE, H_C, DH_C))
    d['cache_fox_logf'] = jax.nn.log_sigmoid(nrm((DEPTH, n_pool, PAGE_SIZE, H_C)) + 2.0)
    d['cache_diff_k'] = nrm((DEPTH, n_pool, PAGE_SIZE, H_D, 2 * DH_D))
    d['cache_diff_v'] = nrm((DEPTH, n_pool, PAGE_SIZE, H_D, 2 * DH_D))
    d['state_conv'] = nrm((DEPTH, DEC_BATCH, CONV_W - 1, W_MIX), 0.5)
    d['state_hgrn'] = nrm((DEPTH, DEC_BATCH, H_B, DK_B, DV_B), 0.3)
    d['page_table'] = jax.random.permutation(next(it), n_pool)[:n_used].reshape(DEC_BATCH, n_pages).astype(jnp.int32)
    d['c_prompt'] = nrm((BATCH, D_MODEL))
    d['c_sample'] = nrm((DEC_BATCH, D_MODEL))
    d['w_ada'] = nrm((DEPTH, D_MODEL, 6 * D_MODEL), 0.5 * D_MODEL ** -0.5)
    d['b_ada'] = nrm((DEPTH, 6 * D_MODEL), 0.02)
    d['norm1_g'] = 1.0 + nrm((DEPTH, D_MODEL), 0.02)
    d['norm2_g'] = 1.0 + nrm((DEPTH, D_MODEL), 0.02)
    d['w_in'] = nrm((DEPTH, D_MODEL, N_IN), D_MODEL ** -0.5)
    d['b_fox_f'] = 2.0 + nrm((DEPTH, H_C), 0.1)
    d['lb_logits'] = nrm((DEPTH, W_MIX), 0.5)
    d['hgrn_norm_g'] = 1.0 + nrm((DEPTH, DV_B), 0.02)
    d['conv_w'] = nrm((DEPTH, CONV_W, W_MIX), CONV_W ** -0.5)
    d['conv_b'] = nrm((DEPTH, W_MIX), 0.02)
    d['conv_ln_g'] = 1.0 + nrm((DEPTH, W_MIX), 0.02)
    d['conv_ln_b'] = nrm((DEPTH, W_MIX), 0.02)
    d['fox_qn_g'] = 1.0 + nrm((DEPTH, DH_C), 0.02)
    d['fox_kn_g'] = 1.0 + nrm((DEPTH, DH_C), 0.02)
    d['diff_qn_g'] = 1.0 + nrm((DEPTH, DH_D), 0.02)
    d['diff_kn_g'] = 1.0 + nrm((DEPTH, DH_D), 0.02)
    d['lam_q1'] = nrm((DEPTH, DH_D), 0.1)
    d['lam_k1'] = nrm((DEPTH, DH_D), 0.1)
    d['lam_q2'] = nrm((DEPTH, DH_D), 0.1)
    d['lam_k2'] = nrm((DEPTH, DH_D), 0.1)
    d['diff_norm_g'] = 1.0 + nrm((DEPTH, 2 * DH_D), 0.02)
    d['w_branch'] = nrm((DEPTH, N_BRANCH, W_MIX, D_MODEL), W_MIX ** -0.5)
    d['w_out'] = nrm((DEPTH, D_MODEL, D_MODEL), D_MODEL ** -0.5)
    d['w_ff1'] = nrm((DEPTH, D_MODEL, D_FF), D_MODEL ** -0.5)
    d['w_ff2'] = nrm((DEPTH, D_FF, D_MODEL), D_FF ** -0.5)
    return d


def reference(x_prompt, x_sample, cache_fox_k, cache_fox_v, cache_fox_logf, cache_diff_k, cache_diff_v,
              state_conv, state_hgrn, page_table, c_prompt, c_sample, w_ada, b_ada, norm1_g, norm2_g, w_in,
              b_fox_f, lb_logits, hgrn_norm_g, conv_w, conv_b, conv_ln_g, conv_ln_b, fox_qn_g, fox_kn_g,
              diff_qn_g, diff_kn_g, lam_q1, lam_k1, lam_q2, lam_k2, diff_norm_g, w_branch, w_out, w_ff1, w_ff2):
    P = dict(w_ada=w_ada, b_ada=b_ada, norm1_g=norm1_g, norm2_g=norm2_g, w_in=w_in, b_fox_f=b_fox_f,
             lb_logits=lb_logits, hgrn_norm_g=hgrn_norm_g, conv_ln_g=conv_ln_g, conv_ln_b=conv_ln_b,
             fox_qn_g=fox_qn_g, fox_kn_g=fox_kn_g, diff_qn_g=diff_qn_g, diff_kn_g=diff_kn_g,
             lam_q1=lam_q1, lam_k1=lam_k1, lam_q2=lam_q2, lam_k2=lam_k2, diff_norm_g=diff_norm_g,
             w_branch=w_branch, w_out=w_out, w_ff1=w_ff1, w_ff2=w_ff2)

    def mix_prompt(l, glu, fb, fc, fd, lam):
        Bt = glu.shape[0]
        gp = jnp.concatenate([jnp.zeros((Bt, CONV_W - 1, W_MIX), glu.dtype), glu], axis=1)
        conv = causal_dwconv(gp, conv_w[l], conv_b[l])
        bq, bk, bv, blogf = fb
        ob, S = hgrn2_recurrence(bq, bk, bv, blogf, jnp.zeros((Bt, H_B, DK_B, DV_B), jnp.float32))
        cq, ck, cv, clogf = fc
        oc = fox_prompt(cq, ck, cv, clogf)
        dq, dk, dv = fd
        od = diff_prompt(dq, dk, dv, lam)
        st = (gp[:, -(CONV_W - 1):], S, ck, cv, clogf, dk.reshape(dk.shape[:3] + (2 * DH_D,)), dv)
        return conv, ob, oc, od, st

    def gather(cache, l):
        g = cache[l, page_table]
        return g.reshape((g.shape[0], g.shape[1] * g.shape[2]) + g.shape[3:])

    def mix_sample(l, glu, fb, fc, fd, lam):
        gp = jnp.concatenate([state_conv[l].astype(glu.dtype), glu], axis=1)
        conv = causal_dwconv(gp, conv_w[l], conv_b[l])
        bq, bk, bv, blogf = fb
        ob, S = hgrn2_recurrence(bq, bk, bv, blogf, state_hgrn[l])
        cq, ck, cv, clogf = fc
        oc = fox_sample(cq, ck, cv, clogf, gather(cache_fox_k, l), gather(cache_fox_v, l), gather(cache_fox_logf, l))
        dq, dk, dv = fd
        kdp = gather(cache_diff_k, l)
        kdp = kdp.reshape(kdp.shape[:3] + (2, DH_D))
        od = diff_sample(dq, dk, dv, kdp, gather(cache_diff_v, l), lam)
        st = (gp[:, -(CONV_W - 1):], S, ck, cv, clogf, dk.reshape(dk.shape[:3] + (2 * DH_D,)), dv)
        return conv, ob, oc, od, st

    y_prompt, ps = _trunk(x_prompt, c_prompt, mix_prompt, P)
    y_sample, ss = _trunk(x_sample, c_sample, mix_sample, P)
    p_conv, p_hgrn, p_fox_k, p_fox_v, p_fox_logf, p_diff_k, p_diff_v = ps
    s_conv, s_hgrn, s_fox_k, s_fox_v, s_fox_logf, s_diff_k, s_diff_v = ss
    return (y_prompt, y_sample, p_conv, p_hgrn, p_fox_k, p_fox_v, p_fox_logf, p_diff_k, p_diff_v,
            s_conv, s_hgrn, s_fox_k, s_fox_v, s_fox_logf, s_diff_k, s_diff_v)
```

```python
import functools
import math

import numpy as np
import jax
import jax.numpy as jnp
from jax import lax
from jax.experimental import pallas as pl
from jax.experimental.pallas import tpu as pltpu

F32 = jnp.float32
BF16 = jnp.bfloat16

EPS = 1e-6
NEG = -1e30
LB_FLOOR = 1e-30
CONV_W = 31
N_HEADS = 4
HGRN_CHUNK = 64
CONV_HALO = 32
ROWS = 8
LANE = 128
V7X_VMEM_LIMIT = 56 * 2**20


def _params(sem, vmem=V7X_VMEM_LIMIT):
    return pltpu.CompilerParams(dimension_semantics=sem, vmem_limit_bytes=vmem)


def _const_spec(shape):
    n = len(shape)
    return pl.BlockSpec(shape, lambda *_: (0,) * n)


def _dot(a, b):
    return jnp.dot(a, b, preferred_element_type=F32)


def _dot_nt(a, b):
    return lax.dot_general(a, b, (((1,), (1,)), ((), ())), preferred_element_type=F32)


def _dot_tn(a, b):
    return lax.dot_general(a, b, (((0,), (0,)), ((), ())), preferred_element_type=F32)


def _split3(a):
    hi = a.astype(BF16)
    r = a - hi.astype(F32)
    mid = r.astype(BF16)
    lo = (r - mid.astype(F32)).astype(BF16)
    return hi, mid, lo


def _dot3(a, m):
    hi, mid, lo = _split3(a)
    return _dot(hi, m) + _dot(mid, m) + _dot(lo, m)


def _dot3_left(m, a):
    hi, mid, lo = _split3(a)
    return _dot(m, hi) + _dot(m, mid) + _dot(m, lo)


def _sigmoid(x):
    return jax.nn.sigmoid(x)


def _silu(x):
    return x * jax.nn.sigmoid(x)


def _log_sigmoid(x):
    return jnp.minimum(x, 0.0) - jnp.log1p(jnp.exp(-jnp.abs(x)))


def _mod_norm(x, g, sc, sh):
    ms = jnp.mean(x * x, axis=-1, keepdims=True)
    return (x * lax.rsqrt(ms + EPS) * g) * (1.0 + sc) + sh


def _group_norm(z, ones_ref, n, gain):
    ss = _dot((z * z).astype(BF16), ones_ref[...]) * (1.0 / n)
    return z * lax.rsqrt(ss + EPS) * gain


def _ada_kernel(c_ref, w_ref, b_ref, o_ref):
    a = _silu(c_ref[...]).astype(BF16)
    o_ref[...] = _dot(a, w_ref[...].astype(BF16)) + b_ref[...]


def _ada(c_all, w_ada, b_ada):
    depth, d, n = w_ada.shape
    rows = c_all.shape[0]
    tn = d
    return pl.pallas_call(
        _ada_kernel,
        grid=(depth, n // tn),
        in_specs=[pl.BlockSpec((rows, d), lambda l, j: (0, 0)),
                  pl.BlockSpec((None, d, tn), lambda l, j: (l, 0, j)),
                  pl.BlockSpec((None, 1, tn), lambda l, j: (l, 0, j))],
        out_specs=pl.BlockSpec((None, rows, tn), lambda l, j: (l, 0, j)),
        out_shape=jax.ShapeDtypeStruct((depth, rows, n), F32),
        compiler_params=_params(("arbitrary", "arbitrary")),
        name="ada_mod",
    )(c_all, w_ada, b_ada.reshape(depth, 1, n))


N_INPROJ_IN = 15


def _inproj_kernel(*refs, w, n_alias):
    (x_ref, sh_ref, sc_ref, g1_ref, wm_ref, wft_ref, bfc_ref,
     oml_ref, dlb_ref, fqn_ref, fkn_ref, dqn_ref, dkn_ref, g64_ref, g32_ref) = refs[:N_INPROJ_IN]
    (glu_o, bq_o, bk_o, bv_o, blf_o, bgs_o, cqb_o, cvb_o, dqb_o, dvb_o,
     ckt_o, cvt_o, dkt_o, dvt_o, cktb_o, dktb_o, clft_o) = refs[N_INPROJ_IN + n_alias:]
    h = _mod_norm(x_ref[...], g1_ref[...], sc_ref[...], sh_ref[...]).astype(BF16)

    def seg(j):
        return _dot(h, wm_ref[:, j * w:(j + 1) * w])

    glu_o[...] = seg(0) * _sigmoid(seg(1))
    bq_o[...] = _silu(seg(2))
    key = oml_ref[...] * (1.0 / (1.0 + jnp.exp(seg(3)))) - dlb_ref[...]
    bk_o[...] = key
    blf_o[...] = jnp.log1p(-key)
    bv_o[...] = seg(4)
    bgs_o[...] = _silu(seg(5))
    cq = _group_norm(seg(6), g64_ref, 64, fqn_ref[...])
    cqb_o[...] = (cq * 0.125).astype(BF16)
    ckt = _group_norm(seg(7), g64_ref, 64, fkn_ref[...]).T
    ckt_o[...] = ckt
    cktb_o[...] = ckt.astype(BF16)
    cv = seg(8)
    cvb_o[...] = cv.astype(BF16)
    cvt_o[...] = cv.T
    clft_o[...] = _log_sigmoid(_dot_nt(wft_ref[...], h) + bfc_ref[...])
    dq = _group_norm(seg(9), g32_ref, 32, dqn_ref[...])
    dqb_o[...] = dq.astype(BF16)
    dkt = _group_norm(seg(10), g32_ref, 32, dkn_ref[...]).T
    dkt_o[...] = dkt
    dktb_o[...] = dkt.astype(BF16)
    dv = seg(11)
    dvb_o[...] = dv.astype(BF16)
    dvt_o[...] = dv.T


def _inproj(x, mod, lw, tm, batch, layer, depth, prev_states):
    t, d = x.shape
    w = d // 4
    s = t // batch
    tpb = s // tm
    r = mod.shape[1]
    row = lambda i: (i, 0)
    f32w = jax.ShapeDtypeStruct((t, w), F32)
    bf16w = jax.ShapeDtypeStruct((t, w), BF16)
    state = jax.ShapeDtypeStruct((depth, batch, w, s), F32)
    wspec = pl.BlockSpec((tm, w), row)
    vec = _const_spec((1, w))
    state_spec = pl.BlockSpec((None, None, w, tm), lambda i: (layer, i // tpb, 0, i % tpb))
    tspec = pl.BlockSpec((None, w, tm), lambda i: (i // tpb, 0, i % tpb))
    out_shapes = ([f32w] * 6 + [bf16w] * 4 + [state] * 4 + [jax.ShapeDtypeStruct((batch, w, s), BF16)] * 2
                  + [jax.ShapeDtypeStruct((ROWS, t), F32)])
    out_specs = [wspec] * 10 + [state_spec] * 4 + [tspec] * 2 + [pl.BlockSpec((ROWS, tm), lambda i: (0, i))]
    in_specs = [pl.BlockSpec((tm, d), row),
                pl.BlockSpec((None, r, d), lambda i: (i // tpb, 0, 0)),
                pl.BlockSpec((None, r, d), lambda i: (i // tpb, 0, 1)),
                _const_spec((1, d)),
                _const_spec((d, 12 * w)), _const_spec((ROWS, d)), _const_spec((ROWS, 1)),
                vec, vec, vec, vec, vec, vec, _const_spec((w, w)), _const_spec((w, w))]
    args = [x, mod, mod, lw["g1"], lw["w_main"], lw["w_ft"], lw["bf_col"],
            lw["one_minus_lb"], lw["dlb"], lw["fqn"], lw["fkn"], lw["dqn"], lw["dkn"], lw["g64"], lw["g32"]]
    assert len(args) == N_INPROJ_IN
    aliases = {}
    for n, a in enumerate(prev_states):
        aliases[len(args)] = 10 + n
        args.append(a)
        in_specs.append(pl.BlockSpec(memory_space=pl.ANY))
    return pl.pallas_call(
        functools.partial(_inproj_kernel, w=w, n_alias=len(aliases)),
        grid=(t // tm,),
        in_specs=in_specs,
        out_specs=out_specs,
        out_shape=out_shapes,
        input_output_aliases=aliases,
        compiler_params=_params(("arbitrary",)),
        name="in_proj",
    )(*args)


def _ln_silu(acc, g, b):
    mu = jnp.mean(acc, axis=-1, keepdims=True)
    cen = acc - mu
    var = jnp.mean(cen * cen, axis=-1, keepdims=True)
    return _silu(cen * lax.rsqrt(var + EPS) * g + b)


def _conv_kernel(cur_ref, prev_ref, w_ref, b_ref, g_ref, beta_ref, o_ref, win, *, tc, sub):
    i = pl.program_id(1)
    halo = prev_ref[tc - CONV_HALO:tc, :]
    win[0:CONV_HALO, :] = jnp.where(i > 0, halo, 0.0)
    win[CONV_HALO:CONV_HALO + tc, :] = cur_ref[...]
    off = CONV_HALO - (CONV_W - 1)
    for r0 in range(0, tc, sub):
        acc = jnp.broadcast_to(b_ref[...], (sub, b_ref.shape[1]))
        for j in range(CONV_W):
            acc = acc + win[r0 + off + j:r0 + off + j + sub, :] * w_ref[j:j + 1, :]
        o_ref[r0:r0 + sub, :] = _ln_silu(acc, g_ref[...], beta_ref[...]).astype(o_ref.dtype)


def _conv_prompt(glu, lw, batch, tc):
    t, w = glu.shape
    npb = t // batch // tc
    vec = _const_spec((1, w))
    return pl.pallas_call(
        functools.partial(_conv_kernel, tc=tc, sub=min(64, tc)),
        grid=(batch, npb),
        in_specs=[pl.BlockSpec((tc, w), lambda b, i: (b * npb + i, 0)),
                  pl.BlockSpec((tc, w), lambda b, i: (b * npb + jnp.maximum(i - 1, 0), 0)),
                  _const_spec((CONV_HALO, w)), vec, vec, vec],
        out_specs=pl.BlockSpec((tc, w), lambda b, i: (b * npb + i, 0)),
        out_shape=jax.ShapeDtypeStruct((t, w), BF16),
        scratch_shapes=[pltpu.VMEM((tc + CONV_HALO, w), F32)],
        compiler_params=_params(("arbitrary", "arbitrary")),
        name="conv_prompt",
    )(glu, glu, lw["conv_w"], lw["conv_b"], lw["conv_ln_g"], lw["conv_ln_b"])


def _hgrn_kernel(q_ref, k_ref, v_ref, lf_ref, gs_ref, gain_ref, tri_ref, bd_ref,
                 y_ref, st_ref, st, e_scr, *, c, w):
    i = pl.program_id(1)

    @pl.when(i == 0)
    def _():
        st[...] = jnp.zeros_like(st)

    q = q_ref[...]
    k = k_ref[...]
    v = v_ref[...]
    bd = bd_ref[...]
    g = _dot3_left(tri_ref[...], lf_ref[...])
    g_last = g[c - 1:c, :]
    o = _dot_nt((q * jnp.exp(g)).astype(BF16), st[...].astype(BF16))
    row = lax.broadcasted_iota(jnp.int32, (8, w), 0)
    offs = []
    pos = 0
    for s in range(c):
        s8 = (s // 8) * 8
        n = c - s8
        d = g[s8:, :] - g[s:s + 1, :]
        qk = q[s8:, :] * k[s:s + 1, :]
        head = jnp.where(row >= s - s8, d[0:8, :], NEG)
        d = head if n == 8 else jnp.concatenate([head, d[8:, :]], axis=0)
        e_scr[pos:pos + n, :] = (jnp.exp(d) * qk).astype(BF16)
        offs.append((pos, s8, n))
        pos += n
    red = _dot(e_scr[...], bd)
    parts = []
    for blk in range(c // 8):
        acc = jnp.zeros((8, w), F32)
        for s in range(min(c, blk * 8 + 8)):
            p0, s8, n = offs[s]
            acc = acc + red[p0 + blk * 8 - s8:p0 + blk * 8 - s8 + 8, :] * v[s:s + 1, :]
        parts.append(acc)
    o = o + jnp.concatenate(parts, axis=0)
    kv = _dot_tn(v.astype(BF16), (k * jnp.exp(g_last - g)).astype(BF16))
    st[...] = st[...] * jnp.exp(g_last) + kv * bd.astype(F32)
    y_ref[...] = (_group_norm(o, bd_ref, 64, gain_ref[...]) * gs_ref[...]).astype(y_ref.dtype)

    @pl.when(i == pl.num_programs(1) - 1)
    def _():
        st_ref[...] = st[...]


def _hgrn_rows(c):
    return sum(c - (s // 8) * 8 for s in range(c))


def _hgrn_prompt(bq, bk, bv, blf, bgs, lw, batch):
    t, w = bq.shape
    c = HGRN_CHUNK
    nc = t // batch // c
    blk = pl.BlockSpec((c, w), lambda b, i: (b * nc + i, 0))
    return pl.pallas_call(
        functools.partial(_hgrn_kernel, c=c, w=w),
        grid=(batch, nc),
        in_specs=[blk, blk, blk, blk, blk, _const_spec((1, w)), _const_spec((c, c)), _const_spec((w, w))],
        out_specs=[blk, pl.BlockSpec((None, w, w), lambda b, i: (b, 0, 0))],
        out_shape=[jax.ShapeDtypeStruct((t, w), BF16), jax.ShapeDtypeStruct((batch, w, w), F32)],
        scratch_shapes=[pltpu.VMEM((w, w), F32), pltpu.VMEM((_hgrn_rows(c), w), BF16)],
        compiler_params=_params(("arbitrary", "arbitrary")),
        name="hgrn_prompt",
    )(bq, bk, bv, blf, bgs, lw["hgrn_g"], lw["tri_hgrn"], lw["g64"])


def _cum_kernel(lf_ref, tri_ref, o_ref, carry):
    i = pl.program_id(1)

    @pl.when(i == 0)
    def _():
        carry[...] = jnp.zeros_like(carry)

    lf = lf_ref[...]
    lf2 = jnp.concatenate([lf, lf], axis=0)
    cum = _dot3(lf2, tri_ref[...])[0:ROWS, :] + carry[:, 0:1]
    o_ref[...] = -cum
    carry[...] = jnp.broadcast_to(cum[:, cum.shape[1] - 1:], carry.shape)


def _neg_cumsum(lft, tri, batch, tcs):
    rows, t = lft.shape
    npb = t // batch // tcs
    blk = pl.BlockSpec((rows, tcs), lambda b, i: (0, b * npb + i))
    return pl.pallas_call(
        _cum_kernel,
        grid=(batch, npb),
        in_specs=[blk, _const_spec((tcs, tcs))],
        out_specs=blk,
        out_shape=jax.ShapeDtypeStruct((rows, t), F32),
        scratch_shapes=[pltpu.VMEM((rows, LANE), F32)],
        compiler_params=_params(("arbitrary", "arbitrary")),
        name="fox_cumsum",
    )(lft, tri)


def _attn_kernel(qi_ref, ki_ref, q_ref, k_ref, v_ref, bq_ref, bk_ref, coef_ref, gain_ref, bd_ref,
                 o_ref, qm, m_scr, l_scr, acc_scr, *, r_rows, diff, scale, t, w):
    j = pl.program_id(1)
    qi = qi_ref[j]
    ki = ki_ref[j]
    width = w // r_rows

    @pl.when(ki == 0)
    def _():
        q = q_ref[...]
        lane = lax.broadcasted_iota(jnp.int32, (t, w), 1) // width
        for r in range(r_rows):
            qm[r] = jnp.where(lane == r, q, jnp.zeros_like(q))
        m_scr[...] = jnp.full_like(m_scr, NEG)
        l_scr[...] = jnp.zeros_like(l_scr)
        acc_scr[...] = jnp.zeros_like(acc_scr)

    def step(masked):
        k = k_ref[...]
        v = v_ref[...]
        if masked:
            keep = lax.broadcasted_iota(jnp.int32, (t, t), 0) >= lax.broadcasted_iota(jnp.int32, (t, t), 1)
        for r in range(r_rows):
            s = _dot(qm[r], k)
            if scale != 1.0:
                s = s * scale
            s = s + (bk_ref[r:r + 1, :] - bq_ref[r:r + 1, 0:1])
            if masked:
                s = jnp.where(keep, s, NEG)
            m_prev = m_scr[r]
            m_new = jnp.maximum(m_prev, jnp.max(s, axis=-1, keepdims=True))
            alpha = jnp.exp(m_prev - m_new)
            p = jnp.exp(s - m_new)
            l_scr[r] = alpha * l_scr[r] + jnp.sum(p, axis=-1, keepdims=True)
            acc_scr[r] = alpha * acc_scr[r] + _dot(p.astype(BF16), v)
            m_scr[r] = m_new

    @pl.when(ki < qi)
    def _():
        step(False)

    @pl.when(ki == qi)
    def _():
        step(True)
        out = jnp.zeros((t, w), F32)
        for r in range(r_rows):
            out = out + acc_scr[r] * (coef_ref[r:r + 1, :] / l_scr[r])
        if diff:
            out = _group_norm(out, bd_ref, 64, gain_ref[...])
        o_ref[...] = out.astype(o_ref.dtype)


def _attn_prompt(q, kt, v, bias, coef, gain, g64, batch, t_blk, r_rows, diff, scale):
    t, w = q.shape
    nq = t // batch // t_blk
    pairs = [(a, b) for a in range(nq) for b in range(a + 1)]
    qi = jnp.asarray([p[0] for p in pairs], jnp.int32)
    ki = jnp.asarray([p[1] for p in pairs], jnp.int32)
    qblk = pl.BlockSpec((t_blk, w), lambda b, j, qi, ki: (b * nq + qi[j], 0))
    vblk = pl.BlockSpec((t_blk, w), lambda b, j, qi, ki: (b * nq + ki[j], 0))
    ktblk = pl.BlockSpec((None, w, t_blk), lambda b, j, qi, ki: (b, 0, ki[j]))
    grid_spec = pltpu.PrefetchScalarGridSpec(
        num_scalar_prefetch=2,
        grid=(batch, len(pairs)),
        in_specs=[qblk, ktblk, vblk,
                  pl.BlockSpec((ROWS, t_blk), lambda b, j, qi, ki: (0, b * nq + qi[j])),
                  pl.BlockSpec((ROWS, t_blk), lambda b, j, qi, ki: (0, b * nq + ki[j])),
                  _const_spec((ROWS, w)), _const_spec((1, w)), _const_spec((w, w))],
        out_specs=qblk,
        scratch_shapes=[pltpu.VMEM((r_rows, t_blk, w), BF16),
                        pltpu.VMEM((r_rows, t_blk, 1), F32),
                        pltpu.VMEM((r_rows, t_blk, 1), F32),
                        pltpu.VMEM((r_rows, t_blk, w), F32)])
    return pl.pallas_call(
        functools.partial(_attn_kernel, r_rows=r_rows, diff=diff, scale=scale, t=t_blk, w=w),
        grid_spec=grid_spec,
        out_shape=jax.ShapeDtypeStruct((t, w), BF16),
        compiler_params=_params(("arbitrary", "arbitrary")),
        name="diff_prompt" if diff else "fox_prompt",
    )(qi, ki, q, kt, v, bias, bias, coef, gain, g64)


def _merge_kernel(x_ref, sh_ref, sc_ref, gt_ref, g1_ref, ya_ref, yb_ref, yc_ref, yd_ref,
                  wg_ref, wb_ref, wo_ref, o_ref, *, d, w):
    x = x_ref[...]
    h = _mod_norm(x, g1_ref[...], sc_ref[...], sh_ref[...]).astype(BF16)
    merged = jnp.zeros(x.shape, F32)
    for j, y_ref in enumerate((ya_ref, yb_ref, yc_ref, yd_ref)):
        gate = _sigmoid(_dot(h, wg_ref[:, j * d:(j + 1) * d]))
        merged = merged + gate * _dot(y_ref[...].astype(BF16), wb_ref[j * w:(j + 1) * w, :])
    o_ref[...] = x + gt_ref[...] * _dot(merged.astype(BF16), wo_ref[...])


def _merge(x, mod, ys, lw, tm, tiles_per_mod):
    t, d = x.shape
    w = d // 4
    r = mod.shape[1]
    row = lambda i: (i, 0)
    modspec = lambda col: pl.BlockSpec((None, r, d), lambda i: (i // tiles_per_mod, 0, col))
    yspec = pl.BlockSpec((tm, w), row)
    return pl.pallas_call(
        functools.partial(_merge_kernel, d=d, w=w),
        grid=(t // tm,),
        in_specs=[pl.BlockSpec((tm, d), row), modspec(0), modspec(1), modspec(2), _const_spec((1, d)),
                  yspec, yspec, yspec, yspec,
                  _const_spec((d, 4 * d)), _const_spec((d, d)), _const_spec((d, d))],
        out_specs=pl.BlockSpec((tm, d), row),
        out_shape=jax.ShapeDtypeStruct((t, d), F32),
        compiler_params=_params(("arbitrary",)),
        name="merge_out",
    )(x, mod, mod, mod, lw["g1"], *ys, lw["w_gates"], lw["w_branch"], lw["w_out"])


def _ffn_kernel(x_ref, sh_ref, sc_ref, gt_ref, g2_ref, w1_ref, w2_ref, o_ref, *, d, n_chunks):
    x = x_ref[...]
    h = _mod_norm(x, g2_ref[...], sc_ref[...], sh_ref[...]).astype(BF16)
    acc = jnp.zeros(x.shape, F32)
    for cidx in range(n_chunks):
        u = jnp.maximum(_dot(h, w1_ref[:, cidx * d:(cidx + 1) * d]), 0.0)
        acc = acc + _dot((u * u).astype(BF16), w2_ref[cidx * d:(cidx + 1) * d, :])
    o_ref[...] = x + gt_ref[...] * acc


def _ffn(x, mod, lw, tm, tiles_per_mod):
    t, d = x.shape
    dff = lw["w_ff1"].shape[1]
    r = mod.shape[1]
    row = lambda i: (i, 0)
    modspec = lambda col: pl.BlockSpec((None, r, d), lambda i: (i // tiles_per_mod, 0, col))
    return pl.pallas_call(
        functools.partial(_ffn_kernel, d=d, n_chunks=dff // d),
        grid=(t // tm,),
        in_specs=[pl.BlockSpec((tm, d), row), modspec(3), modspec(4), modspec(5), _const_spec((1, d)),
                  _const_spec((d, dff)), _const_spec((dff, d))],
        out_specs=pl.BlockSpec((tm, d), row),
        out_shape=jax.ShapeDtypeStruct((t, d), F32),
        compiler_params=_params(("arbitrary",)),
        name="ffn",
    )(x, mod, mod, mod, lw["g2"], lw["w_ff1"], lw["w_ff2"])


def _conv_step_kernel(state_ref, glu_ref, w_ref, b_ref, g_ref, beta_ref, o_ref):
    n = CONV_W - 1
    acc = glu_ref[...] * w_ref[n:n + 1, :] + b_ref[...]
    for j in range(n):
        acc = acc + state_ref[j] * w_ref[j:j + 1, :]
    o_ref[...] = _ln_silu(acc, g_ref[...], beta_ref[...])


def _conv_step(state_t, glu, lw):
    db, w = glu.shape
    return pl.pallas_call(
        _conv_step_kernel,
        out_shape=jax.ShapeDtypeStruct((db, w), F32),
        compiler_params=_params(None),
        name="conv_step",
    )(state_t, glu, lw["conv_w"], lw["conv_b"], lw["conv_ln_g"], lw["conv_ln_b"])


def _hgrn_step_kernel(s0_ref, qc_ref, kc_ref, lfc_ref, vr_ref, gsr_ref, gain_ref, y_ref, s_ref):
    s_new = jnp.exp(lfc_ref[...]) * s0_ref[...] + kc_ref[...] * vr_ref[...]
    s_ref[...] = s_new
    o = jnp.sum(qc_ref[...] * s_new, axis=2, keepdims=True)
    ms = jnp.mean(o * o, axis=-1, keepdims=True)
    y_ref[...] = o * lax.rsqrt(ms + EPS) * gain_ref[...] * gsr_ref[...]


def _hgrn_step(s0, bq, bk, bv, blf, bgs, gain):
    db, nh, dk, dv = s0.shape
    col = lambda a: a.reshape(db, nh, dk, 1)
    rowv = lambda a: a.reshape(db, nh, 1, dv)
    y, s = pl.pallas_call(
        _hgrn_step_kernel,
        out_shape=[jax.ShapeDtypeStruct((db, nh, 1, dv), F32), jax.ShapeDtypeStruct(s0.shape, F32)],
        compiler_params=_params(None),
        name="hgrn_step",
    )(s0, col(bq), col(bk), col(blf), rowv(bv), rowv(bgs), gain.reshape(1, 1, 1, dv))
    return y.reshape(db, nh * dv), s


def _suffix_kernel(lf_ref, m_ref, o_ref):
    o_ref[...] = _dot3(lf_ref[...], m_ref[...])


def _suffix_matrix(page):
    src = np.arange(page)[:, None]
    dst = np.arange(2 * page)[None, :]
    return jnp.asarray((dst >= page) | (src > dst), BF16)


def _suffix_table(logf_rows, page):
    n = logf_rows.shape[0]
    tr = math.gcd(n, 1024)
    out = pl.pallas_call(
        _suffix_kernel,
        grid=(n // tr,),
        in_specs=[pl.BlockSpec((tr, page), lambda i: (i, 0)), _const_spec((page, 2 * page))],
        out_specs=pl.BlockSpec((tr, 2 * page), lambda i: (i, 0)),
        out_shape=jax.ShapeDtypeStruct((n, 2 * page), F32),
        compiler_params=_params(("arbitrary",)),
        name="fox_suffix_table",
    )(logf_rows, _suffix_matrix(page))
    return out.reshape(n // ROWS, ROWS, 2 * page)


def _decode_kernel(pt_ref, qm_ref, kn_ref, vn_ref, cnew_ref, slope_ref, coef_ref, gain_ref, bd_ref,
                   kc_ref, vc_ref, wt_ref, o_ref,
                   kbuf, vbuf, wbuf, sem, m_scr, l_scr, acc_scr, carry,
                   *, fox, diff_norm, scale, g_pages, n_groups, n_seq, page, nbuf, page_base, w):
    n_it = n_seq * n_groups
    past = n_groups * g_pages * page

    def copies(it, slot):
        b = it // n_groups
        c = n_groups - 1 - it % n_groups
        out = []
        for g in range(g_pages):
            pg = page_base + pt_ref[b, c * g_pages + g]
            out.append(pltpu.make_async_copy(kc_ref.at[pg], kbuf.at[slot, g], sem.at[0, slot]))
            out.append(pltpu.make_async_copy(vc_ref.at[pg], vbuf.at[slot, g], sem.at[1, slot]))
            if fox:
                out.append(pltpu.make_async_copy(wt_ref.at[pg], wbuf.at[slot, g], sem.at[2, slot]))
        return out

    for it0 in range(min(nbuf - 1, n_it)):
        for cp in copies(it0, it0 % nbuf):
            cp.start()

    def body(it, _):
        slot = it % nbuf
        b = it // n_groups
        c = n_groups - 1 - it % n_groups

        @pl.when(it + nbuf - 1 < n_it)
        def _():
            for cp in copies(it + nbuf - 1, (it + nbuf - 1) % nbuf):
                cp.start()

        qm = qm_ref[b]

        @pl.when(c == n_groups - 1)
        def _():
            m_scr[...] = jnp.sum(qm * kn_ref[b], axis=-1, keepdims=True) * scale
            l_scr[...] = jnp.ones_like(l_scr)
            acc_scr[...] = jnp.broadcast_to(vn_ref[b], acc_scr.shape)
            carry[...] = jnp.zeros_like(carry)

        for cp in copies(it, slot):
            cp.wait()

        s = jnp.concatenate([_dot(qm, kbuf[slot, g]) for g in range(g_pages)], axis=1) * scale
        if fox:
            parts = [None] * g_pages
            run = carry[...]
            for g in reversed(range(g_pages)):
                parts[g] = wbuf[slot, g, :, 0:page] + run
                run = run + wbuf[slot, g, :, page:2 * page]
            carry[...] = run
            s = s + jnp.concatenate(parts, axis=1) + cnew_ref[b]
        else:
            pos = (c * (g_pages * page) + lax.broadcasted_iota(jnp.int32, (1, g_pages * page), 1)).astype(F32)
            s = s + slope_ref[...] * (pos - float(past))
        m_prev = m_scr[...]
        m_new = jnp.maximum(m_prev, jnp.max(s, axis=-1, keepdims=True))
        alpha = jnp.exp(m_prev - m_new)
        p = jnp.exp(s - m_new)
        l_scr[...] = alpha * l_scr[...] + jnp.sum(p, axis=-1, keepdims=True)
        pv = _dot_nt(p[:, 0:page], vbuf[slot, 0])
        for g in range(1, g_pages):
            pv = pv + _dot_nt(p[:, g * page:(g + 1) * page], vbuf[slot, g])
        acc_scr[...] = alpha * acc_scr[...] + pv
        m_scr[...] = m_new

        @pl.when(c == 0)
        def _():
            out = jnp.sum(acc_scr[...] * (coef_ref[...] / l_scr[...]), axis=0, keepdims=True)
            if diff_norm:
                out8 = jnp.broadcast_to(out, (ROWS, w))
                out = _group_norm(out8, bd_ref, 64, gain_ref[...])[0:1, :]
            o_ref[b] = out

        return 0

    lax.fori_loop(0, n_it, body, 0)


def _decode_attn(page_table, qm, k_new, v_new, cnew, slopes, coef, gain, g64, kt_cache, vt_cache, w_table,
                 layer, n_pool, fox, scale):
    db, n_pages = page_table.shape
    w, page = kt_cache.shape[1], kt_cache.shape[2]
    g_pages = math.gcd(n_pages, 8)
    nbuf = 3
    n_groups = n_pages // g_pages
    full3 = lambda shape: pl.BlockSpec(shape, lambda i, pt: (0, 0, 0))
    full2 = lambda shape: pl.BlockSpec(shape, lambda i, pt: (0, 0))
    any_spec = pl.BlockSpec(memory_space=pl.ANY)
    grid_spec = pltpu.PrefetchScalarGridSpec(
        num_scalar_prefetch=1,
        grid=(1,),
        in_specs=[full3((db, ROWS, w)), full3((db, 1, w)), full3((db, 1, w)), full3((db, ROWS, 1)),
                  full2((ROWS, 1)), full2((ROWS, w)), full2((1, w)), full2((w, w)),
                  any_spec, any_spec, any_spec],
        out_specs=full3((db, 1, w)),
        scratch_shapes=[pltpu.VMEM((nbuf, g_pages, w, page), F32),
                        pltpu.VMEM((nbuf, g_pages, w, page), F32),
                        pltpu.VMEM((nbuf, g_pages, ROWS, 2 * page), F32),
                        pltpu.SemaphoreType.DMA((3, nbuf)),
                        pltpu.VMEM((ROWS, 1), F32), pltpu.VMEM((ROWS, 1), F32),
                        pltpu.VMEM((ROWS, w), F32), pltpu.VMEM((ROWS, page), F32)])
    out = pl.pallas_call(
        functools.partial(_decode_kernel, fox=fox, diff_norm=not fox, scale=scale, g_pages=g_pages,
                          n_groups=n_groups, n_seq=db, page=page, nbuf=nbuf, page_base=layer * n_pool, w=w),
        grid_spec=grid_spec,
        out_shape=jax.ShapeDtypeStruct((db, 1, w), F32),
        compiler_params=_params(("arbitrary",)),
        name="fox_decode" if fox else "diff_decode",
    )(page_table, qm, k_new.reshape(db, 1, w), v_new.reshape(db, 1, w), cnew, slopes, coef, gain, g64,
      kt_cache, vt_cache, w_table)
    return out.reshape(db, w)


def _block_diag_ones(w, n):
    idx = np.arange(w) // n
    return jnp.asarray(idx[:, None] == idx[None, :], BF16)


def _row_mask(w, r_rows):
    lane = np.arange(w) // (w // r_rows)
    return jnp.asarray(np.arange(ROWS)[:, None] == lane[None, :], F32)


def _alibi_slopes(n):
    return jnp.asarray(2.0 ** (-8.0 * np.arange(1, n + 1) / n), F32)


def _layer_weights(l, p, lbs, d):
    w = d // 4
    w_in = p["w_in"][l]
    o_f = 9 * w
    o_d = o_f + N_HEADS
    o_g = o_d + 3 * w
    w_f = w_in[:, o_f:o_d]
    lb = lbs[l]
    lbf = jnp.maximum(lb, LB_FLOOR)
    tile = lambda a, n: jnp.tile(a, n).reshape(1, w)
    conv_w = jnp.concatenate([p["conv_w"][l], jnp.zeros((CONV_HALO - CONV_W, w), F32)], axis=0)
    lam_init = 0.8 - 0.6 * math.exp(-0.3 * l)
    lam = (jnp.exp(jnp.sum(p["lam_q1"][l] * p["lam_k1"][l])) - jnp.exp(jnp.sum(p["lam_q2"][l] * p["lam_k2"][l]))
           + lam_init).astype(F32)
    mask8 = jnp.repeat(_row_mask(w, N_HEADS)[:N_HEADS], 2, axis=0)
    sign = jnp.where(jnp.arange(ROWS) % 2 == 0, 1.0, -lam).astype(F32)
    return dict(
        g1=p["norm1_g"][l].reshape(1, d), g2=p["norm2_g"][l].reshape(1, d),
        w_main=jnp.concatenate([w_in[:, :o_f], w_in[:, o_d:o_g]], axis=1).astype(BF16),
        w_ft=jnp.pad(w_f.T, ((0, ROWS - N_HEADS), (0, 0))).astype(BF16),
        bf_col=jnp.pad(p["b_fox_f"][l], (0, ROWS - N_HEADS)).reshape(ROWS, 1),
        one_minus_lb=(1.0 - lb).reshape(1, w), dlb=(lbf - lb).reshape(1, w),
        fqn=tile(p["fox_qn_g"][l], N_HEADS), fkn=tile(p["fox_kn_g"][l], N_HEADS),
        dqn=tile(p["diff_qn_g"][l], 2 * N_HEADS), dkn=tile(p["diff_kn_g"][l], 2 * N_HEADS),
        hgrn_g=tile(p["hgrn_norm_g"][l], N_HEADS), hgrn_g1=p["hgrn_norm_g"][l],
        diff_g=tile(p["diff_norm_g"][l], N_HEADS) * (1.0 - lam_init),
        conv_w=conv_w, conv_b=p["conv_b"][l].reshape(1, w),
        conv_ln_g=p["conv_ln_g"][l].reshape(1, w), conv_ln_b=p["conv_ln_b"][l].reshape(1, w),
        w_gates=w_in[:, o_g:].astype(BF16),
        w_branch=p["w_branch"][l].reshape(d, d).astype(BF16),
        w_out=p["w_out"][l].astype(BF16),
        w_ff1=p["w_ff1"][l].astype(BF16), w_ff2=p["w_ff2"][l].astype(BF16),
        fox_coef=_row_mask(w, N_HEADS), diff_coef=mask8 * sign[:, None],
        ones_w=jnp.ones((1, w), F32),
    )


def _hgrn_state_from_transposed(st, w):
    b = st.shape[0]
    dh = w // N_HEADS
    st5 = st.reshape(b, N_HEADS, dh, N_HEADS, dh)
    return jnp.stack([st5[:, h, :, h, :] for h in range(N_HEADS)], axis=1).transpose(0, 1, 3, 2)


def kernel(x_prompt, x_sample, cache_fox_k, cache_fox_v, cache_fox_logf, cache_diff_k, cache_diff_v,
           state_conv, state_hgrn, page_table, c_prompt, c_sample, w_ada, b_ada, norm1_g, norm2_g, w_in,
           b_fox_f, lb_logits, hgrn_norm_g, conv_w, conv_b, conv_ln_g, conv_ln_b, fox_qn_g, fox_kn_g,
           diff_qn_g, diff_kn_g, lam_q1, lam_k1, lam_q2, lam_k2, diff_norm_g, w_branch, w_out, w_ff1, w_ff2):
    p = dict(norm1_g=norm1_g, norm2_g=norm2_g, w_in=w_in, b_fox_f=b_fox_f, hgrn_norm_g=hgrn_norm_g,
             conv_w=conv_w, conv_b=conv_b, conv_ln_g=conv_ln_g, conv_ln_b=conv_ln_b, fox_qn_g=fox_qn_g,
             fox_kn_g=fox_kn_g, diff_qn_g=diff_qn_g, diff_kn_g=diff_kn_g, lam_q1=lam_q1, lam_k1=lam_k1,
             lam_q2=lam_q2, lam_k2=lam_k2, diff_norm_g=diff_norm_g, w_branch=w_branch, w_out=w_out,
             w_ff1=w_ff1, w_ff2=w_ff2)
    batch, seq, d = x_prompt.shape
    db = x_sample.shape[0]
    assert x_sample.shape[1] == 1, "the sample group advances one token per step"
    depth = w_in.shape[0]
    w = d // 4
    dh = w // N_HEADS
    n_pool, page = cache_fox_k.shape[1], cache_fox_k.shape[2]
    n_pages = page_table.shape[1]

    lbs = jax.nn.softmax(lb_logits.astype(F32), axis=0)
    lbs = jnp.cumsum(lbs, axis=0) - lbs[0]
    lws = [_layer_weights(l, p, lbs, d) for l in range(depth)]
    g64 = _block_diag_ones(w, dh)
    g32 = _block_diag_ones(w, dh // 2)
    for lw in lws:
        lw["g64"], lw["g32"] = g64, g32

    rows = -(-(batch + db) // 8) * 8
    c_all = jnp.concatenate([c_prompt, c_sample, jnp.zeros((rows - batch - db, d), F32)], axis=0)
    mod = _ada(c_all, w_ada, b_ada)
    mod_p = mod[:, :batch].reshape(depth, batch, 1, 6 * d)
    mod_s = mod[:, batch:batch + db].reshape(depth, 1, db, 6 * d)

    t = batch * seq
    tm = math.gcd(seq, 512)
    t_attn = math.gcd(seq, 256)
    tcs = math.gcd(seq, 512)
    tri_cum = jnp.asarray(np.triu(np.ones((tcs, tcs))), BF16)
    tri_hgrn = jnp.asarray(np.tril(np.ones((HGRN_CHUNK, HGRN_CHUNK))), BF16)
    slopes = _alibi_slopes(N_HEADS)
    pos = jnp.arange(t, dtype=F32) % seq
    alibi_bias = jnp.repeat(slopes, 2)[:, None] * pos[None, :]

    def cache_layout(a):
        return a.reshape(a.shape[0], a.shape[1], N_HEADS, dh, a.shape[3]).transpose(0, 1, 4, 2, 3)

    x = x_prompt.reshape(t, d)
    kv_p = [jnp.zeros((depth, batch, w, seq), F32) for _ in range(4)]
    p_conv, p_hgrn, p_logf = [], [], []
    for l, lw in enumerate(lws):
        lw["tri_hgrn"] = tri_hgrn
        outs = _inproj(x, mod_p[l], lw, tm, batch, l, depth, kv_p)
        glu, bq, bk, bv, blf, bgs, cqb, cvb, dqb, dvb = outs[:10]
        kv_p = outs[10:14]
        cktb, dktb, clft = outs[14:]
        ya = _conv_prompt(glu, lw, batch, math.gcd(seq, 256))
        yb, st = _hgrn_prompt(bq, bk, bv, blf, bgs, lw, batch)
        fox_bias = _neg_cumsum(clft, tri_cum, batch, tcs)
        yc = _attn_prompt(cqb, cktb, cvb, fox_bias, lw["fox_coef"], lw["ones_w"], g64, batch, t_attn,
                          N_HEADS, False, 1.0)
        yd = _attn_prompt(dqb, dktb, dvb, alibi_bias, lw["diff_coef"], lw["diff_g"], g64, batch, t_attn,
                          2 * N_HEADS, True, (dh // 2) ** -0.5)
        x = _merge(x, mod_p[l], (ya, yb, yc, yd), lw, tm, seq // tm)
        x = _ffn(x, mod_p[l], lw, tm, seq // tm)
        p_conv.append(glu.reshape(batch, seq, w)[:, seq - (CONV_W - 1):])
        p_hgrn.append(_hgrn_state_from_transposed(st, w))
        p_logf.append(clft[:N_HEADS].reshape(N_HEADS, batch, seq).transpose(1, 2, 0))
    y_prompt = x.reshape(batch, seq, d)
    p_out = [jnp.stack(p_conv), jnp.stack(p_hgrn), cache_layout(kv_p[0]), cache_layout(kv_p[1]),
             jnp.stack(p_logf), cache_layout(kv_p[2]), cache_layout(kv_p[3])]

    pool_t = lambda a: a.transpose(0, 1, 3, 4, 2).reshape(depth * n_pool, w, page)
    fox_kt, fox_vt = pool_t(cache_fox_k), pool_t(cache_fox_v)
    diff_kt, diff_vt = pool_t(cache_diff_k), pool_t(cache_diff_v)
    logf_rows = jnp.pad(cache_fox_logf.transpose(0, 1, 3, 2), ((0, 0), (0, 0), (0, ROWS - N_HEADS), (0, 0)))
    w_table = _suffix_table(logf_rows.reshape(depth * n_pool * ROWS, page), page)
    mask4 = _row_mask(w, N_HEADS)
    mask8 = _row_mask(w, 2 * N_HEADS)
    slope_col = jnp.repeat(slopes, 2).reshape(ROWS, 1)
    zero_col = jnp.zeros((db, ROWS, 1), F32)
    head_rows = jnp.arange(ROWS)[:, None] < N_HEADS
    conv_t = state_conv.transpose(0, 2, 1, 3)

    x = x_sample.reshape(db, d)
    kv_s = [jnp.zeros((depth, 1, w, db), F32) for _ in range(4)]
    s_conv, s_hgrn, s_logf = [], [], []
    for l, lw in enumerate(lws):
        outs = _inproj(x, mod_s[l], lw, db, 1, l, depth, kv_s)
        glu, bq, bk, bv, blf, bgs, cqb, cvb, dqb, dvb = outs[:10]
        kv_s = outs[10:14]
        clft = outs[16]
        ck, cv, dk, dv = [a[l, 0].T for a in kv_s]
        ya = _conv_step(conv_t[l], glu, lw)
        yb, s_new = _hgrn_step(state_hgrn[l], bq, bk, bv, blf, bgs, lw["hgrn_g1"])
        cnew = jnp.where(head_rows, clft, 0.0).T.reshape(db, ROWS, 1)
        qm_fox = cqb.astype(F32)[:, None, :] * mask4[None]
        yc = _decode_attn(page_table, qm_fox, ck, cv, cnew, slope_col, lw["fox_coef"], lw["ones_w"], g64,
                          fox_kt, fox_vt, w_table, l, n_pool, True, 1.0)
        qm_diff = dqb.astype(F32)[:, None, :] * mask8[None]
        yd = _decode_attn(page_table, qm_diff, dk, dv, zero_col, slope_col, lw["diff_coef"], lw["diff_g"], g64,
                          diff_kt, diff_vt, w_table, l, n_pool, False, (dh // 2) ** -0.5)
        x = _merge(x, mod_s[l], (ya, yb, yc, yd), lw, db, 1)
        x = _ffn(x, mod_s[l], lw, db, 1)
        s_conv.append(jnp.concatenate([conv_t[l][1:], glu[None]], axis=0).transpose(1, 0, 2))
        s_hgrn.append(s_new)
        s_logf.append(clft[:N_HEADS].T.reshape(db, 1, N_HEADS))
    y_sample = x.reshape(db, 1, d)
    sample_layout = lambda a: cache_layout(a).transpose(0, 2, 1, 3, 4)
    s_out = [jnp.stack(s_conv), jnp.stack(s_hgrn), sample_layout(kv_s[0]), sample_layout(kv_s[1]),
             jnp.stack(s_logf), sample_layout(kv_s[2]), sample_layout(kv_s[3])]
    return (y_prompt, y_sample, *p_out, *s_out)
```

```python
import functools
import math

import numpy as np
import jax
import jax.numpy as jnp
from jax import lax
from jax.experimental import pallas as pl
from jax.experimental.pallas import tpu as pltpu

F32 = jnp.float32
BF16 = jnp.bfloat16

EPS = 1e-6
NEG = -1e30
LOG2E = math.log2(math.e)
LB_FLOOR = 1e-30
CONV_W = 31
N_HEADS = 4
HGRN_CHUNK = 64
CONV_HALO = 32
ROWS = 8
LANE = 128
V7X_VMEM_LIMIT = 56 * 2**20


def _params(sem, vmem=V7X_VMEM_LIMIT):
    return pltpu.CompilerParams(dimension_semantics=sem, vmem_limit_bytes=vmem)


def _const_spec(shape):
    n = len(shape)
    return pl.BlockSpec(shape, lambda *_: (0,) * n)


def _dot(a, b):
    return jnp.dot(a, b, preferred_element_type=F32)


def _dot_nt(a, b):
    return lax.dot_general(a, b, (((1,), (1,)), ((), ())), preferred_element_type=F32)


def _dot_tn(a, b):
    return lax.dot_general(a, b, (((0,), (0,)), ((), ())), preferred_element_type=F32)


def _split3(a):
    hi = a.astype(BF16)
    r = a - hi.astype(F32)
    mid = r.astype(BF16)
    lo = (r - mid.astype(F32)).astype(BF16)
    return hi, mid, lo


def _dot3(a, m):
    hi, mid, lo = _split3(a)
    return _dot(hi, m) + _dot(mid, m) + _dot(lo, m)


def _dot3_left(m, a):
    hi, mid, lo = _split3(a)
    return _dot(m, hi) + _dot(m, mid) + _dot(m, lo)


def _sigmoid(x):
    return jax.nn.sigmoid(x)


def _silu(x):
    return x * jax.nn.sigmoid(x)


def _log_sigmoid(x):
    return jnp.minimum(x, 0.0) - jnp.log1p(jnp.exp(-jnp.abs(x)))


def _mod_norm(x, g, sc, sh):
    ms = jnp.mean(x * x, axis=-1, keepdims=True)
    return (x * lax.rsqrt(ms + EPS) * g) * (1.0 + sc) + sh


def _group_norm(z, ones_ref, n, gain):
    ss = _dot((z * z).astype(BF16), ones_ref[...]) * (1.0 / n)
    return z * lax.rsqrt(ss + EPS) * gain


def _ada_kernel(c_ref, w_ref, b_ref, o_ref):
    a = _silu(c_ref[...]).astype(BF16)
    o_ref[...] = _dot(a, w_ref[...].astype(BF16)) + b_ref[...]


def _ada(c_all, w_ada, b_ada):
    depth, d, n = w_ada.shape
    rows = c_all.shape[0]
    tn = d
    return pl.pallas_call(
        _ada_kernel,
        grid=(depth, n // tn),
        in_specs=[pl.BlockSpec((rows, d), lambda l, j: (0, 0)),
                  pl.BlockSpec((None, d, tn), lambda l, j: (l, 0, j)),
                  pl.BlockSpec((None, 1, tn), lambda l, j: (l, 0, j))],
        out_specs=pl.BlockSpec((None, rows, tn), lambda l, j: (l, 0, j)),
        out_shape=jax.ShapeDtypeStruct((depth, rows, n), F32),
        compiler_params=_params(("arbitrary", "arbitrary")),
        name="ada_mod",
    )(c_all, w_ada, b_ada.reshape(depth, 1, n))


W_PREP_ROWS = 256


def _wprep_kernel(w_ref, o_ref):
    o_ref[...] = w_ref[...].T.astype(BF16)


def _weight_prep(wt, row0, n_rows):
    d = wt.shape[1]
    bn = W_PREP_ROWS
    assert row0 % bn == 0 and n_rows % bn == 0
    return pl.pallas_call(
        _wprep_kernel,
        grid=(n_rows // bn,),
        in_specs=[pl.BlockSpec((bn, d), lambda i: (row0 // bn + i, 0))],
        out_specs=pl.BlockSpec((d, bn), lambda i: (0, i)),
        out_shape=jax.ShapeDtypeStruct((d, n_rows), BF16),
        compiler_params=_params(("arbitrary",)),
        name="weight_prep",
    )(wt)


N_INPROJ_IN = 16
N_INPROJ_STATE0 = 8


def _inproj_kernel(*refs, w, n_alias):
    (x_ref, sh_ref, sc_ref, g1_ref, wabc_ref, wd_ref, wft_ref, bfc_ref,
     oml_ref, dlb_ref, fqn_ref, fkn_ref, dqn_ref, dkn_ref, g64_ref, g32_ref) = refs[:N_INPROJ_IN]
    (glu_o, bq_o, bk_o, bv_o, blf_o, bgs_o, ckb_o, dkb_o,
     ckt_o, cvt_o, dkt_o, dvt_o, cqtb_o, cvtb_o, dqtb_o, dvtb_o, clft_o) = refs[N_INPROJ_IN + n_alias:]
    h = _mod_norm(x_ref[...], g1_ref[...], sc_ref[...], sh_ref[...]).astype(BF16)

    def seg(j):
        if j < 9:
            return _dot(h, wabc_ref[:, j * w:(j + 1) * w])
        return _dot(h, wd_ref[:, (j - 9) * w:(j - 8) * w])

    glu_o[...] = seg(0) * _sigmoid(seg(1))
    bq_o[...] = _silu(seg(2))
    key = oml_ref[...] * (1.0 / (1.0 + jnp.exp(seg(3)))) - dlb_ref[...]
    bk_o[...] = key
    blf_o[...] = jnp.log1p(-key)
    bv_o[...] = seg(4)
    bgs_o[...] = _silu(seg(5))
    cq = _group_norm(seg(6), g64_ref, 64, fqn_ref[...])
    cqtb_o[...] = (cq * (64 ** -0.5 * LOG2E)).T.astype(BF16)
    ck = _group_norm(seg(7), g64_ref, 64, fkn_ref[...])
    ckb_o[...] = ck.astype(BF16)
    ckt_o[...] = ck.T
    cvt = seg(8).T
    cvt_o[...] = cvt
    cvtb_o[...] = cvt.astype(BF16)
    clft_o[...] = _log_sigmoid(_dot_nt(wft_ref[...], h) + bfc_ref[...])
    dq = _group_norm(seg(9), g32_ref, 32, dqn_ref[...])
    dqtb_o[...] = (dq * (32 ** -0.5 * LOG2E)).T.astype(BF16)
    dk = _group_norm(seg(10), g32_ref, 32, dkn_ref[...])
    dkb_o[...] = dk.astype(BF16)
    dkt_o[...] = dk.T
    dvt = seg(11).T
    dvt_o[...] = dvt
    dvtb_o[...] = dvt.astype(BF16)


def _inproj(x, mod, lw, tm, batch, layer, depth, prev_states):
    t, d = x.shape
    w = d // 4
    s = t // batch
    tpb = s // tm
    r = mod.shape[1]
    row = lambda i: (i, 0)
    f32w = jax.ShapeDtypeStruct((t, w), F32)
    bf16w = jax.ShapeDtypeStruct((t, w), BF16)
    state = jax.ShapeDtypeStruct((depth, batch, w, s), F32)
    wspec = pl.BlockSpec((tm, w), row)
    vec = _const_spec((1, w))
    state_spec = pl.BlockSpec((None, None, w, tm), lambda i: (layer, i // tpb, 0, i % tpb))
    tspec = pl.BlockSpec((None, w, tm), lambda i: (i // tpb, 0, i % tpb))
    out_shapes = ([f32w] * 6 + [bf16w] * 2 + [state] * 4 + [jax.ShapeDtypeStruct((batch, w, s), BF16)] * 4
                  + [jax.ShapeDtypeStruct((ROWS, t), F32)])
    out_specs = [wspec] * 8 + [state_spec] * 4 + [tspec] * 4 + [pl.BlockSpec((ROWS, tm), lambda i: (0, i))]
    in_specs = [pl.BlockSpec((tm, d), row),
                pl.BlockSpec((None, r, d), lambda i: (i // tpb, 0, 0)),
                pl.BlockSpec((None, r, d), lambda i: (i // tpb, 0, 1)),
                _const_spec((1, d)),
                _const_spec((d, 9 * w)), _const_spec((d, 3 * w)),
                _const_spec((ROWS, d)), _const_spec((ROWS, 1)),
                vec, vec, vec, vec, vec, vec, _const_spec((w, w)), _const_spec((w, w))]
    args = [x, mod, mod, lw["g1"], lw["w_abc"], lw["w_d"], lw["w_ft"], lw["bf_col"],
            lw["one_minus_lb"], lw["dlb"], lw["fqn"], lw["fkn"], lw["dqn"], lw["dkn"], lw["g64"], lw["g32"]]
    assert len(args) == N_INPROJ_IN
    aliases = {}
    for n, a in enumerate(prev_states):
        aliases[len(args)] = N_INPROJ_STATE0 + n
        args.append(a)
        in_specs.append(pl.BlockSpec(memory_space=pl.ANY))
    return pl.pallas_call(
        functools.partial(_inproj_kernel, w=w, n_alias=len(aliases)),
        grid=(t // tm,),
        in_specs=in_specs,
        out_specs=out_specs,
        out_shape=out_shapes,
        input_output_aliases=aliases,
        compiler_params=_params(("arbitrary",)),
        name="in_proj",
    )(*args)


def _ln_silu(acc, g, b):
    mu = jnp.mean(acc, axis=-1, keepdims=True)
    cen = acc - mu
    var = jnp.mean(cen * cen, axis=-1, keepdims=True)
    return _silu(cen * lax.rsqrt(var + EPS) * g + b)


def _conv_kernel(cur_ref, prev_ref, w_ref, b_ref, g_ref, beta_ref, o_ref, win, *, tc, sub):
    i = pl.program_id(1)
    halo = prev_ref[tc - CONV_HALO:tc, :]
    win[0:CONV_HALO, :] = jnp.where(i > 0, halo, 0.0)
    win[CONV_HALO:CONV_HALO + tc, :] = cur_ref[...]
    off = CONV_HALO - (CONV_W - 1)
    for r0 in range(0, tc, sub):
        acc = jnp.broadcast_to(b_ref[...], (sub, b_ref.shape[1]))
        for j in range(CONV_W):
            acc = acc + win[r0 + off + j:r0 + off + j + sub, :] * w_ref[j:j + 1, :]
        o_ref[r0:r0 + sub, :] = _ln_silu(acc, g_ref[...], beta_ref[...]).astype(o_ref.dtype)


def _conv_prompt(glu, lw, batch, tc):
    t, w = glu.shape
    npb = t // batch // tc
    vec = _const_spec((1, w))
    return pl.pallas_call(
        functools.partial(_conv_kernel, tc=tc, sub=min(64, tc)),
        grid=(batch, npb),
        in_specs=[pl.BlockSpec((tc, w), lambda b, i: (b * npb + i, 0)),
                  pl.BlockSpec((tc, w), lambda b, i: (b * npb + jnp.maximum(i - 1, 0), 0)),
                  _const_spec((CONV_HALO, w)), vec, vec, vec],
        out_specs=pl.BlockSpec((tc, w), lambda b, i: (b * npb + i, 0)),
        out_shape=jax.ShapeDtypeStruct((t, w), BF16),
        scratch_shapes=[pltpu.VMEM((tc + CONV_HALO, w), F32)],
        compiler_params=_params(("arbitrary", "arbitrary")),
        name="conv_prompt",
    )(glu, glu, lw["conv_w"], lw["conv_b"], lw["conv_ln_g"], lw["conv_ln_b"])


def _hgrn_kernel(q_ref, k_ref, v_ref, lf_ref, gs_ref, gain_ref, tri_ref, bd_ref,
                 y_ref, st_ref, st, e_scr, *, c, w):
    i = pl.program_id(1)

    @pl.when(i == 0)
    def _():
        st[...] = jnp.zeros_like(st)

    q = q_ref[...]
    k = k_ref[...]
    v = v_ref[...]
    bd = bd_ref[...]
    g = _dot3_left(tri_ref[...], lf_ref[...])
    g_last = g[c - 1:c, :]
    o = _dot_nt((q * jnp.exp(g)).astype(BF16), st[...].astype(BF16))
    row = lax.broadcasted_iota(jnp.int32, (8, w), 0)
    offs = []
    pos = 0
    for s in range(c):
        s8 = (s // 8) * 8
        n = c - s8
        d = g[s8:, :] - g[s:s + 1, :]
        qk = q[s8:, :] * k[s:s + 1, :]
        head = jnp.where(row >= s - s8, d[0:8, :], NEG)
        d = head if n == 8 else jnp.concatenate([head, d[8:, :]], axis=0)
        e_scr[pos:pos + n, :] = (jnp.exp(d) * qk).astype(BF16)
        offs.append((pos, s8, n))
        pos += n
    red = _dot(e_scr[...], bd)
    parts = []
    for blk in range(c // 8):
        acc = jnp.zeros((8, w), F32)
        for s in range(min(c, blk * 8 + 8)):
            p0, s8, n = offs[s]
            acc = acc + red[p0 + blk * 8 - s8:p0 + blk * 8 - s8 + 8, :] * v[s:s + 1, :]
        parts.append(acc)
    o = o + jnp.concatenate(parts, axis=0)
    kv = _dot_tn(v.astype(BF16), (k * jnp.exp(g_last - g)).astype(BF16))
    st[...] = st[...] * jnp.exp(g_last) + kv * bd.astype(F32)
    y_ref[...] = (_group_norm(o, bd_ref, 64, gain_ref[...]) * gs_ref[...]).astype(y_ref.dtype)

    @pl.when(i == pl.num_programs(1) - 1)
    def _():
        st_ref[...] = st[...]


def _hgrn_rows(c):
    return sum(c - (s // 8) * 8 for s in range(c))


def _hgrn_prompt(bq, bk, bv, blf, bgs, lw, batch):
    t, w = bq.shape
    c = HGRN_CHUNK
    nc = t // batch // c
    blk = pl.BlockSpec((c, w), lambda b, i: (b * nc + i, 0))
    return pl.pallas_call(
        functools.partial(_hgrn_kernel, c=c, w=w),
        grid=(batch, nc),
        in_specs=[blk, blk, blk, blk, blk, _const_spec((1, w)), _const_spec((c, c)), _const_spec((w, w))],
        out_specs=[blk, pl.BlockSpec((None, w, w), lambda b, i: (b, 0, 0))],
        out_shape=[jax.ShapeDtypeStruct((t, w), BF16), jax.ShapeDtypeStruct((batch, w, w), F32)],
        scratch_shapes=[pltpu.VMEM((w, w), F32), pltpu.VMEM((_hgrn_rows(c), w), BF16)],
        compiler_params=_params(("arbitrary", "arbitrary")),
        name="hgrn_prompt",
    )(bq, bk, bv, blf, bgs, lw["hgrn_g"], lw["tri_hgrn"], lw["g64"])


def _cum_kernel(lf_ref, tri_ref, o_ref, carry):
    i = pl.program_id(1)

    @pl.when(i == 0)
    def _():
        carry[...] = jnp.zeros_like(carry)

    lf = lf_ref[...]
    lf2 = jnp.concatenate([lf, lf], axis=0)
    cum = _dot3(lf2, tri_ref[...])[0:ROWS, :] + carry[:, 0:1]
    carry[...] = jnp.broadcast_to(cum[:, cum.shape[1] - 1:], carry.shape)
    pad = jnp.zeros((LANE - ROWS, cum.shape[1]), F32)
    o_ref[...] = jnp.concatenate([-cum, pad], axis=0).T


def _neg_cumsum(lft, tri, batch, tcs):
    rows, t = lft.shape
    npb = t // batch // tcs
    return pl.pallas_call(
        _cum_kernel,
        grid=(batch, npb),
        in_specs=[pl.BlockSpec((rows, tcs), lambda b, i: (0, b * npb + i)), _const_spec((tcs, tcs))],
        out_specs=pl.BlockSpec((tcs, LANE), lambda b, i: (b * npb + i, 0)),
        out_shape=jax.ShapeDtypeStruct((t, LANE), F32),
        scratch_shapes=[pltpu.VMEM((rows, LANE), F32)],
        compiler_params=_params(("arbitrary", "arbitrary")),
        name="fox_cumsum",
    )(lft, tri)


ATTN_QW = 256
ATTN_KC = 128


def _attn_kernel(qi_ref, ki_ref, qt_ref, k_ref, vt_ref, bq_ref, bk_ref, lam_ref, gain_ref, bd_ref,
                 o_ref, qtm, m_scr, l_scr, acc_scr, bias_scr, s_scr, p_scr, *, r_rows, diff, slopes, t, w):
    j = pl.program_id(1)
    qi = qi_ref[j]
    ki = ki_ref[j]
    dh = w // N_HEADS
    width = w // r_rows
    qw, kc_len = min(ATTN_QW, t), min(ATTN_KC, t)
    key_iota = lax.broadcasted_iota(jnp.int32, (kc_len, qw), 0)

    @pl.when(ki == 0)
    def _():
        qt = qt_ref[...]
        owner = lax.broadcasted_iota(jnp.int32, (w, t), 0) // width
        for r in range(r_rows):
            qtm[r] = jnp.where(owner == r, qt, jnp.zeros_like(qt))
        m_scr[...] = jnp.full_like(m_scr, NEG)
        l_scr[...] = jnp.zeros_like(l_scr)
        acc_scr[...] = jnp.zeros_like(acc_scr)
        if diff:
            for h in range(N_HEADS):
                bias_scr[h] = (slopes[h] * LOG2E) * key_iota.astype(F32)

    def step(masked):
        if diff:
            tile_off = ((ki - qi) * t).astype(F32)
        else:
            for h in range(N_HEADS):
                col = (bk_ref[:, h:h + 1] - bq_ref[0:1, h:h + 1]) * LOG2E
                bias_scr[h] = jnp.broadcast_to(col, (t, LANE))
        group = 0
        for r in range(r_rows):
            h = r * N_HEADS // r_rows
            for q0 in range(0, t, qw):
                buf = group % 2
                group += 1
                n_keys = min(t, q0 + qw) if masked else t
                chunks = range(0, n_keys, kc_len)
                shifts = [slopes[h] * LOG2E * (tile_off + float(k0)) if diff else 0.0 for k0 in chunks]
                m_run = m_scr[r, :, q0:q0 + qw]
                m_new = m_run
                for k0, shift in zip(chunks, shifts):
                    s = _dot(k_ref[k0:k0 + kc_len, :], qtm[r, :, q0:q0 + qw])
                    if diff:
                        s = s + bias_scr[h]
                    else:
                        s = s + jnp.concatenate([bias_scr[h, k0:k0 + kc_len, :]] * (qw // LANE), axis=1)
                    if masked and k0 + kc_len - 1 > q0:
                        q_iota = lax.broadcasted_iota(jnp.int32, (kc_len, qw), 1)
                        s = jnp.where(key_iota + (k0 - q0) <= q_iota, s, NEG)
                    s_scr[buf, k0:k0 + kc_len, :] = s
                    m_new = jnp.maximum(m_new, jnp.max(s, axis=0, keepdims=True) + shift)
                alpha = jnp.exp2(m_run - m_new)
                l_add = jnp.zeros((1, qw), F32)
                for k0, shift in zip(chunks, shifts):
                    p = jnp.exp2(s_scr[buf, k0:k0 + kc_len, :] - (m_new - shift))
                    l_add = l_add + jnp.sum(p, axis=0, keepdims=True)
                    p_scr[buf, k0:k0 + kc_len, :] = p.astype(BF16)
                pv = _dot(vt_ref[h * dh:(h + 1) * dh, 0:n_keys], p_scr[buf, 0:n_keys, :])
                m_scr[r, :, q0:q0 + qw] = m_new
                l_scr[r, :, q0:q0 + qw] = alpha * l_scr[r, :, q0:q0 + qw] + l_add
                acc_scr[r, :, q0:q0 + qw] = alpha * acc_scr[r, :, q0:q0 + qw] + pv

    @pl.when(ki < qi)
    def _():
        step(False)

    @pl.when(ki == qi)
    def _():
        step(True)
        heads = []
        for h in range(N_HEADS):
            if diff:
                o_h = (acc_scr[2 * h] * (1.0 / l_scr[2 * h])
                       - lam_ref[...] * (acc_scr[2 * h + 1] * (1.0 / l_scr[2 * h + 1])))
            else:
                o_h = acc_scr[h] * (1.0 / l_scr[h])
            heads.append(o_h)
        out = jnp.concatenate(heads, axis=0).T
        if diff:
            out = _group_norm(out, bd_ref, dh, gain_ref[...])
        o_ref[...] = out.astype(o_ref.dtype)


def _attn_prompt(qt, k, vt, bias, lam, gain, g64, batch, t_blk, r_rows, diff, slopes):
    w = qt.shape[1]
    t = k.shape[0]
    nq = t // batch // t_blk
    pairs = [(a, b) for a in range(nq) for b in range(a + 1)]
    qi = jnp.asarray([p[0] for p in pairs], jnp.int32)
    ki = jnp.asarray([p[1] for p in pairs], jnp.int32)
    oblk = pl.BlockSpec((t_blk, w), lambda b, j, qi, ki: (b * nq + qi[j], 0))
    kblk = pl.BlockSpec((t_blk, w), lambda b, j, qi, ki: (b * nq + ki[j], 0))
    qtblk = pl.BlockSpec((None, w, t_blk), lambda b, j, qi, ki: (b, 0, qi[j]))
    vtblk = pl.BlockSpec((None, w, t_blk), lambda b, j, qi, ki: (b, 0, ki[j]))
    if diff:
        bias = jnp.zeros((ROWS, LANE), F32)
        bias_specs = [_const_spec((ROWS, LANE))] * 2
        bias_scr = pltpu.VMEM((N_HEADS, min(ATTN_KC, t_blk), min(ATTN_QW, t_blk)), F32)
    else:
        bias_specs = [pl.BlockSpec((t_blk, LANE), lambda b, j, qi, ki: (b * nq + qi[j], 0)),
                      pl.BlockSpec((t_blk, LANE), lambda b, j, qi, ki: (b * nq + ki[j], 0))]
        bias_scr = pltpu.VMEM((N_HEADS, t_blk, LANE), F32)
    grid_spec = pltpu.PrefetchScalarGridSpec(
        num_scalar_prefetch=2,
        grid=(batch, len(pairs)),
        in_specs=[qtblk, kblk, vtblk, *bias_specs,
                  _const_spec((1, 1)), _const_spec((1, w)), _const_spec((w, w))],
        out_specs=oblk,
        scratch_shapes=[pltpu.VMEM((r_rows, w, t_blk), BF16),
                        pltpu.VMEM((r_rows, 1, t_blk), F32),
                        pltpu.VMEM((r_rows, 1, t_blk), F32),
                        pltpu.VMEM((r_rows, w // N_HEADS, t_blk), F32),
                        bias_scr,
                        pltpu.VMEM((2, t_blk, min(ATTN_QW, t_blk)), F32),
                        pltpu.VMEM((2, t_blk, min(ATTN_QW, t_blk)), BF16)])
    return pl.pallas_call(
        functools.partial(_attn_kernel, r_rows=r_rows, diff=diff, slopes=slopes, t=t_blk, w=w),
        grid_spec=grid_spec,
        out_shape=jax.ShapeDtypeStruct((t, w), BF16),
        compiler_params=_params(("arbitrary", "arbitrary")),
        name="diff_prompt" if diff else "fox_prompt",
    )(qi, ki, qt, k, vt, bias, bias, lam, gain, g64)


def _merge_kernel(x_ref, sh_ref, sc_ref, gt_ref, g1_ref, ya_ref, yb_ref, yc_ref, yd_ref,
                  wg_ref, wb_ref, wo_ref, o_ref, *, d, w):
    x = x_ref[...]
    h = _mod_norm(x, g1_ref[...], sc_ref[...], sh_ref[...]).astype(BF16)
    merged = jnp.zeros(x.shape, F32)
    for j, y_ref in enumerate((ya_ref, yb_ref, yc_ref, yd_ref)):
        gate = _sigmoid(_dot(h, wg_ref[:, j * d:(j + 1) * d]))
        merged = merged + gate * _dot(y_ref[...].astype(BF16), wb_ref[j * w:(j + 1) * w, :])
    o_ref[...] = x + gt_ref[...] * _dot(merged.astype(BF16), wo_ref[...])


def _merge(x, mod, ys, lw, tm, tiles_per_mod):
    t, d = x.shape
    w = d // 4
    r = mod.shape[1]
    row = lambda i: (i, 0)
    modspec = lambda col: pl.BlockSpec((None, r, d), lambda i: (i // tiles_per_mod, 0, col))
    yspec = pl.BlockSpec((tm, w), row)
    return pl.pallas_call(
        functools.partial(_merge_kernel, d=d, w=w),
        grid=(t // tm,),
        in_specs=[pl.BlockSpec((tm, d), row), modspec(0), modspec(1), modspec(2), _const_spec((1, d)),
                  yspec, yspec, yspec, yspec,
                  _const_spec((d, 4 * d)), _const_spec((d, d)), _const_spec((d, d))],
        out_specs=pl.BlockSpec((tm, d), row),
        out_shape=jax.ShapeDtypeStruct((t, d), F32),
        compiler_params=_params(("arbitrary",)),
        name="merge_out",
    )(x, mod, mod, mod, lw["g1"], *ys, lw["w_gates"], lw["w_branch"], lw["w_out"])


def _ffn_kernel(x_ref, sh_ref, sc_ref, gt_ref, g2_ref, w1_ref, w2_ref, o_ref, *, d, n_chunks):
    x = x_ref[...]
    h = _mod_norm(x, g2_ref[...], sc_ref[...], sh_ref[...]).astype(BF16)
    acc = jnp.zeros(x.shape, F32)
    for cidx in range(n_chunks):
        u = jnp.maximum(_dot(h, w1_ref[:, cidx * d:(cidx + 1) * d]), 0.0)
        acc = acc + _dot((u * u).astype(BF16), w2_ref[cidx * d:(cidx + 1) * d, :])
    o_ref[...] = x + gt_ref[...] * acc


def _ffn(x, mod, lw, tm, tiles_per_mod):
    t, d = x.shape
    dff = lw["w_ff1"].shape[1]
    r = mod.shape[1]
    row = lambda i: (i, 0)
    modspec = lambda col: pl.BlockSpec((None, r, d), lambda i: (i // tiles_per_mod, 0, col))
    return pl.pallas_call(
        functools.partial(_ffn_kernel, d=d, n_chunks=dff // d),
        grid=(t // tm,),
        in_specs=[pl.BlockSpec((tm, d), row), modspec(3), modspec(4), modspec(5), _const_spec((1, d)),
                  _const_spec((d, dff)), _const_spec((dff, d))],
        out_specs=pl.BlockSpec((tm, d), row),
        out_shape=jax.ShapeDtypeStruct((t, d), F32),
        compiler_params=_params(("arbitrary",)),
        name="ffn",
    )(x, mod, mod, mod, lw["g2"], lw["w_ff1"], lw["w_ff2"])


def _conv_step_kernel(state_ref, glu_ref, w_ref, b_ref, g_ref, beta_ref, o_ref):
    n = CONV_W - 1
    acc = glu_ref[...] * w_ref[n:n + 1, :] + b_ref[...]
    for j in range(n):
        acc = acc + state_ref[j] * w_ref[j:j + 1, :]
    o_ref[...] = _ln_silu(acc, g_ref[...], beta_ref[...])


def _conv_step(state_t, glu, lw):
    db, w = glu.shape
    return pl.pallas_call(
        _conv_step_kernel,
        out_shape=jax.ShapeDtypeStruct((db, w), F32),
        compiler_params=_params(None),
        name="conv_step",
    )(state_t, glu, lw["conv_w"], lw["conv_b"], lw["conv_ln_g"], lw["conv_ln_b"])


def _hgrn_step_kernel(s0_ref, qc_ref, kc_ref, lfc_ref, vr_ref, gsr_ref, gain_ref, y_ref, s_ref):
    s_new = jnp.exp(lfc_ref[...]) * s0_ref[...] + kc_ref[...] * vr_ref[...]
    s_ref[...] = s_new
    o = jnp.sum(qc_ref[...] * s_new, axis=2, keepdims=True)
    ms = jnp.mean(o * o, axis=-1, keepdims=True)
    y_ref[...] = o * lax.rsqrt(ms + EPS) * gain_ref[...] * gsr_ref[...]


def _hgrn_step(s0, bq, bk, bv, blf, bgs, gain):
    db, nh, dk, dv = s0.shape
    col = lambda a: a.reshape(db, nh, dk, 1)
    rowv = lambda a: a.reshape(db, nh, 1, dv)
    y, s = pl.pallas_call(
        _hgrn_step_kernel,
        out_shape=[jax.ShapeDtypeStruct((db, nh, 1, dv), F32), jax.ShapeDtypeStruct(s0.shape, F32)],
        compiler_params=_params(None),
        name="hgrn_step",
    )(s0, col(bq), col(bk), col(blf), rowv(bv), rowv(bgs), gain.reshape(1, 1, 1, dv))
    return y.reshape(db, nh * dv), s


def _suffix_kernel(lf_ref, m_ref, o_ref):
    o_ref[...] = _dot3(lf_ref[...], m_ref[...])


def _suffix_matrix(page):
    src = np.arange(page)[:, None]
    dst = np.arange(2 * page)[None, :]
    return jnp.asarray((dst >= page) | (src > dst), BF16)


def _suffix_table(logf_rows, page):
    n = logf_rows.shape[0]
    tr = math.gcd(n, 1024)
    out = pl.pallas_call(
        _suffix_kernel,
        grid=(n // tr,),
        in_specs=[pl.BlockSpec((tr, page), lambda i: (i, 0)), _const_spec((page, 2 * page))],
        out_specs=pl.BlockSpec((tr, 2 * page), lambda i: (i, 0)),
        out_shape=jax.ShapeDtypeStruct((n, 2 * page), F32),
        compiler_params=_params(("arbitrary",)),
        name="fox_suffix_table",
    )(logf_rows, _suffix_matrix(page))
    return out.reshape(n // ROWS, ROWS, 2 * page)


def _decode_kernel(pt_ref, qm_ref, kn_ref, vn_ref, cnew_ref, slope_ref, coef_ref, gain_ref, bd_ref,
                   kc_ref, vc_ref, wt_ref, o_ref,
                   kbuf, vbuf, wbuf, sem, m_scr, l_scr, acc_scr, carry,
                   *, fox, diff_norm, scale, g_pages, n_groups, n_seq, page, nbuf, page_base, w):
    n_it = n_seq * n_groups
    past = n_groups * g_pages * page

    def copies(it, slot):
        b = it // n_groups
        c = n_groups - 1 - it % n_groups
        out = []
        for g in range(g_pages):
            pg = page_base + pt_ref[b, c * g_pages + g]
            out.append(pltpu.make_async_copy(kc_ref.at[pg], kbuf.at[slot, g], sem.at[0, slot]))
            out.append(pltpu.make_async_copy(vc_ref.at[pg], vbuf.at[slot, g], sem.at[1, slot]))
            if fox:
                out.append(pltpu.make_async_copy(wt_ref.at[pg], wbuf.at[slot, g], sem.at[2, slot]))
        return out

    for it0 in range(min(nbuf - 1, n_it)):
        for cp in copies(it0, it0 % nbuf):
            cp.start()

    def body(it, _):
        slot = it % nbuf
        b = it // n_groups
        c = n_groups - 1 - it % n_groups

        @pl.when(it + nbuf - 1 < n_it)
        def _():
            for cp in copies(it + nbuf - 1, (it + nbuf - 1) % nbuf):
                cp.start()

        qm = qm_ref[b]

        @pl.when(c == n_groups - 1)
        def _():
            m_scr[...] = jnp.sum(qm * kn_ref[b], axis=-1, keepdims=True) * scale
            l_scr[...] = jnp.ones_like(l_scr)
            acc_scr[...] = jnp.broadcast_to(vn_ref[b], acc_scr.shape)
            carry[...] = jnp.zeros_like(carry)

        for cp in copies(it, slot):
            cp.wait()

        s = jnp.concatenate([_dot(qm, kbuf[slot, g]) for g in range(g_pages)], axis=1) * scale
        if fox:
            parts = [None] * g_pages
            run = carry[...]
            for g in reversed(range(g_pages)):
                parts[g] = wbuf[slot, g, :, 0:page] + run
                run = run + wbuf[slot, g, :, page:2 * page]
            carry[...] = run
            s = s + jnp.concatenate(parts, axis=1) + cnew_ref[b]
        else:
            pos = (c * (g_pages * page) + lax.broadcasted_iota(jnp.int32, (1, g_pages * page), 1)).astype(F32)
            s = s + slope_ref[...] * (pos - float(past))
        m_prev = m_scr[...]
        m_new = jnp.maximum(m_prev, jnp.max(s, axis=-1, keepdims=True))
        alpha = jnp.exp(m_prev - m_new)
        p = jnp.exp(s - m_new)
        l_scr[...] = alpha * l_scr[...] + jnp.sum(p, axis=-1, keepdims=True)
        pv = _dot_nt(p[:, 0:page], vbuf[slot, 0])
        for g in range(1, g_pages):
            pv = pv + _dot_nt(p[:, g * page:(g + 1) * page], vbuf[slot, g])
        acc_scr[...] = alpha * acc_scr[...] + pv
        m_scr[...] = m_new

        @pl.when(c == 0)
        def _():
            out = jnp.sum(acc_scr[...] * (coef_ref[...] / l_scr[...]), axis=0, keepdims=True)
            if diff_norm:
                out8 = jnp.broadcast_to(out, (ROWS, w))
                out = _group_norm(out8, bd_ref, 64, gain_ref[...])[0:1, :]
            o_ref[b] = out

        return 0

    lax.fori_loop(0, n_it, body, 0)


def _decode_attn(page_table, qm, k_new, v_new, cnew, slopes, coef, gain, g64, kt_cache, vt_cache, w_table,
                 layer, n_pool, fox, scale):
    db, n_pages = page_table.shape
    w, page = kt_cache.shape[1], kt_cache.shape[2]
    g_pages = math.gcd(n_pages, 8)
    nbuf = 3
    n_groups = n_pages // g_pages
    full3 = lambda shape: pl.BlockSpec(shape, lambda i, pt: (0, 0, 0))
    full2 = lambda shape: pl.BlockSpec(shape, lambda i, pt: (0, 0))
    any_spec = pl.BlockSpec(memory_space=pl.ANY)
    grid_spec = pltpu.PrefetchScalarGridSpec(
        num_scalar_prefetch=1,
        grid=(1,),
        in_specs=[full3((db, ROWS, w)), full3((db, 1, w)), full3((db, 1, w)), full3((db, ROWS, 1)),
                  full2((ROWS, 1)), full2((ROWS, w)), full2((1, w)), full2((w, w)),
                  any_spec, any_spec, any_spec],
        out_specs=full3((db, 1, w)),
        scratch_shapes=[pltpu.VMEM((nbuf, g_pages, w, page), F32),
                        pltpu.VMEM((nbuf, g_pages, w, page), F32),
                        pltpu.VMEM((nbuf, g_pages, ROWS, 2 * page), F32),
                        pltpu.SemaphoreType.DMA((3, nbuf)),
                        pltpu.VMEM((ROWS, 1), F32), pltpu.VMEM((ROWS, 1), F32),
                        pltpu.VMEM((ROWS, w), F32), pltpu.VMEM((ROWS, page), F32)])
    out = pl.pallas_call(
        functools.partial(_decode_kernel, fox=fox, diff_norm=not fox, scale=scale, g_pages=g_pages,
                          n_groups=n_groups, n_seq=db, page=page, nbuf=nbuf, page_base=layer * n_pool, w=w),
        grid_spec=grid_spec,
        out_shape=jax.ShapeDtypeStruct((db, 1, w), F32),
        compiler_params=_params(("arbitrary",)),
        name="fox_decode" if fox else "diff_decode",
    )(page_table, qm, k_new.reshape(db, 1, w), v_new.reshape(db, 1, w), cnew, slopes, coef, gain, g64,
      kt_cache, vt_cache, w_table)
    return out.reshape(db, w)


def _block_diag_ones(w, n):
    idx = np.arange(w) // n
    return jnp.asarray(idx[:, None] == idx[None, :], BF16)


def _row_mask(w, r_rows):
    lane = np.arange(w) // (w // r_rows)
    return jnp.asarray(np.arange(ROWS)[:, None] == lane[None, :], F32)


def _alibi_slopes(n):
    return jnp.asarray(2.0 ** (-8.0 * np.arange(1, n + 1) / n), F32)


def _layer_weights(l, p, lbs, d):
    w = d // 4
    o_f = 9 * w
    o_d = o_f + N_HEADS
    w_in_t = p["w_in"].transpose(2, 0, 1)
    wt_abc = w_in_t[:o_f, l]
    wt_f = w_in_t[o_f:o_d, l]
    wt_rest = w_in_t[o_d:, l]
    lb = lbs[l]
    lbf = jnp.maximum(lb, LB_FLOOR)
    tile = lambda a, n: jnp.tile(a, n).reshape(1, w)
    conv_w = jnp.concatenate([p["conv_w"][l], jnp.zeros((CONV_HALO - CONV_W, w), F32)], axis=0)
    lam_init = 0.8 - 0.6 * math.exp(-0.3 * l)
    lam = (jnp.exp(jnp.sum(p["lam_q1"][l] * p["lam_k1"][l])) - jnp.exp(jnp.sum(p["lam_q2"][l] * p["lam_k2"][l]))
           + lam_init).astype(F32)
    mask8 = jnp.repeat(_row_mask(w, N_HEADS)[:N_HEADS], 2, axis=0)
    sign = jnp.where(jnp.arange(ROWS) % 2 == 0, 1.0, -lam).astype(F32)
    return dict(
        g1=p["norm1_g"][l].reshape(1, d), g2=p["norm2_g"][l].reshape(1, d),
        w_abc=_weight_prep(wt_abc, 0, o_f), w_d=_weight_prep(wt_rest, 0, 3 * w),
        w_gates=_weight_prep(wt_rest, 3 * w, 4 * d),
        w_ft=jnp.pad(wt_f, ((0, ROWS - N_HEADS), (0, 0))).astype(BF16),
        bf_col=jnp.pad(p["b_fox_f"][l], (0, ROWS - N_HEADS)).reshape(ROWS, 1),
        one_minus_lb=(1.0 - lb).reshape(1, w), dlb=(lbf - lb).reshape(1, w),
        fqn=tile(p["fox_qn_g"][l], N_HEADS), fkn=tile(p["fox_kn_g"][l], N_HEADS),
        dqn=tile(p["diff_qn_g"][l], 2 * N_HEADS), dkn=tile(p["diff_kn_g"][l], 2 * N_HEADS),
        hgrn_g=tile(p["hgrn_norm_g"][l], N_HEADS), hgrn_g1=p["hgrn_norm_g"][l],
        diff_g=tile(p["diff_norm_g"][l], N_HEADS) * (1.0 - lam_init),
        conv_w=conv_w, conv_b=p["conv_b"][l].reshape(1, w),
        conv_ln_g=p["conv_ln_g"][l].reshape(1, w), conv_ln_b=p["conv_ln_b"][l].reshape(1, w),
        lam=lam.reshape(1, 1),
        w_branch=p["w_branch"][l].reshape(d, d).astype(BF16),
        w_out=p["w_out"][l].astype(BF16),
        w_ff1=p["w_ff1"][l].astype(BF16), w_ff2=p["w_ff2"][l].astype(BF16),
        fox_coef=_row_mask(w, N_HEADS), diff_coef=mask8 * sign[:, None],
        ones_w=jnp.ones((1, w), F32),
    )


def _hgrn_state_from_transposed(st, w):
    b = st.shape[0]
    dh = w // N_HEADS
    st5 = st.reshape(b, N_HEADS, dh, N_HEADS, dh)
    return jnp.stack([st5[:, h, :, h, :] for h in range(N_HEADS)], axis=1).transpose(0, 1, 3, 2)


def kernel(x_prompt, x_sample, cache_fox_k, cache_fox_v, cache_fox_logf, cache_diff_k, cache_diff_v,
           state_conv, state_hgrn, page_table, c_prompt, c_sample, w_ada, b_ada, norm1_g, norm2_g, w_in,
           b_fox_f, lb_logits, hgrn_norm_g, conv_w, conv_b, conv_ln_g, conv_ln_b, fox_qn_g, fox_kn_g,
           diff_qn_g, diff_kn_g, lam_q1, lam_k1, lam_q2, lam_k2, diff_norm_g, w_branch, w_out, w_ff1, w_ff2):
    p = dict(norm1_g=norm1_g, norm2_g=norm2_g, w_in=w_in, b_fox_f=b_fox_f, hgrn_norm_g=hgrn_norm_g,
             conv_w=conv_w, conv_b=conv_b, conv_ln_g=conv_ln_g, conv_ln_b=conv_ln_b, fox_qn_g=fox_qn_g,
             fox_kn_g=fox_kn_g, diff_qn_g=diff_qn_g, diff_kn_g=diff_kn_g, lam_q1=lam_q1, lam_k1=lam_k1,
             lam_q2=lam_q2, lam_k2=lam_k2, diff_norm_g=diff_norm_g, w_branch=w_branch, w_out=w_out,
             w_ff1=w_ff1, w_ff2=w_ff2)
    batch, seq, d = x_prompt.shape
    db = x_sample.shape[0]
    assert x_sample.shape[1] == 1, "the sample group advances one token per step"
    depth = w_in.shape[0]
    w = d // 4
    dh = w // N_HEADS
    n_pool, page = cache_fox_k.shape[1], cache_fox_k.shape[2]
    n_pages = page_table.shape[1]

    lbs = jax.nn.softmax(lb_logits.astype(F32), axis=0)
    lbs = jnp.cumsum(lbs, axis=0) - lbs[0]
    lws = [_layer_weights(l, p, lbs, d) for l in range(depth)]
    g64 = _block_diag_ones(w, dh)
    g32 = _block_diag_ones(w, dh // 2)
    for lw in lws:
        lw["g64"], lw["g32"] = g64, g32

    rows = -(-(batch + db) // 8) * 8
    c_all = jnp.concatenate([c_prompt, c_sample, jnp.zeros((rows - batch - db, d), F32)], axis=0)
    mod = _ada(c_all, w_ada, b_ada)
    mod_p = mod[:, :batch].reshape(depth, batch, 1, 6 * d)
    mod_s = mod[:, batch:batch + db].reshape(depth, 1, db, 6 * d)

    t = batch * seq
    tm = math.gcd(seq, 512)
    t_attn = math.gcd(seq, 512)
    tcs = math.gcd(seq, 512)
    tri_cum = jnp.asarray(np.triu(np.ones((tcs, tcs))), BF16)
    tri_hgrn = jnp.asarray(np.tril(np.ones((HGRN_CHUNK, HGRN_CHUNK))), BF16)
    slope_vals = tuple(float(2.0 ** (-8.0 * (h + 1) / N_HEADS)) for h in range(N_HEADS))
    slopes = jnp.asarray(slope_vals, F32)

    def cache_layout(a):
        return a.reshape(a.shape[0], a.shape[1], N_HEADS, dh, a.shape[3]).transpose(0, 1, 4, 2, 3)

    x = x_prompt.reshape(t, d)
    kv_p = [jnp.zeros((depth, batch, w, seq), F32) for _ in range(4)]
    p_conv, p_hgrn, p_logf = [], [], []
    for l, lw in enumerate(lws):
        lw["tri_hgrn"] = tri_hgrn
        outs = _inproj(x, mod_p[l], lw, tm, batch, l, depth, kv_p)
        glu, bq, bk, bv, blf, bgs, ckb, dkb = outs[:8]
        kv_p = outs[8:12]
        cqtb, cvtb, dqtb, dvtb, clft = outs[12:]
        ya = _conv_prompt(glu, lw, batch, math.gcd(seq, 256))
        yb, st = _hgrn_prompt(bq, bk, bv, blf, bgs, lw, batch)
        fox_bias = _neg_cumsum(clft, tri_cum, batch, tcs)
        yc = _attn_prompt(cqtb, ckb, cvtb, fox_bias, lw["lam"], lw["ones_w"], g64, batch, t_attn,
                          N_HEADS, False, slope_vals)
        yd = _attn_prompt(dqtb, dkb, dvtb, None, lw["lam"], lw["diff_g"], g64, batch, t_attn,
                          2 * N_HEADS, True, slope_vals)
        x = _merge(x, mod_p[l], (ya, yb, yc, yd), lw, tm, seq // tm)
        x = _ffn(x, mod_p[l], lw, tm, seq // tm)
        p_conv.append(glu.reshape(batch, seq, w)[:, seq - (CONV_W - 1):])
        p_hgrn.append(_hgrn_state_from_transposed(st, w))
        p_logf.append(clft[:N_HEADS].reshape(N_HEADS, batch, seq).transpose(1, 2, 0))
    y_prompt = x.reshape(batch, seq, d)
    p_out = [jnp.stack(p_conv), jnp.stack(p_hgrn), cache_layout(kv_p[0]), cache_layout(kv_p[1]),
             jnp.stack(p_logf), cache_layout(kv_p[2]), cache_layout(kv_p[3])]

    pool_t = lambda a: a.transpose(0, 1, 3, 4, 2).reshape(depth * n_pool, w, page)
    fox_kt, fox_vt = pool_t(cache_fox_k), pool_t(cache_fox_v)
    diff_kt, diff_vt = pool_t(cache_diff_k), pool_t(cache_diff_v)
    logf_rows = jnp.pad(cache_fox_logf.transpose(0, 1, 3, 2), ((0, 0), (0, 0), (0, ROWS - N_HEADS), (0, 0)))
    w_table = _suffix_table(logf_rows.reshape(depth * n_pool * ROWS, page), page)
    mask4 = _row_mask(w, N_HEADS)
    mask8 = _row_mask(w, 2 * N_HEADS)
    slope_col = jnp.repeat(slopes, 2).reshape(ROWS, 1)
    zero_col = jnp.zeros((db, ROWS, 1), F32)
    head_rows = jnp.arange(ROWS)[:, None] < N_HEADS
    conv_t = state_conv.transpose(0, 2, 1, 3)

    x = x_sample.reshape(db, d)
    kv_s = [jnp.zeros((depth, 1, w, db), F32) for _ in range(4)]
    s_conv, s_hgrn, s_logf = [], [], []
    for l, lw in enumerate(lws):
        outs = _inproj(x, mod_s[l], lw, db, 1, l, depth, kv_s)
        glu, bq, bk, bv, blf, bgs = outs[:6]
        kv_s = outs[8:12]
        cqb, dqb = outs[12][0].T, outs[14][0].T
        clft = outs[16]
        ck, cv, dk, dv = [a[l, 0].T for a in kv_s]
        ya = _conv_step(conv_t[l], glu, lw)
        yb, s_new = _hgrn_step(state_hgrn[l], bq, bk, bv, blf, bgs, lw["hgrn_g1"])
        cnew = jnp.where(head_rows, clft, 0.0).T.reshape(db, ROWS, 1)
        qm_fox = (cqb.astype(F32) * (1.0 / LOG2E))[:, None, :] * mask4[None]
        yc = _decode_attn(page_table, qm_fox, ck, cv, cnew, slope_col, lw["fox_coef"], lw["ones_w"], g64,
                          fox_kt, fox_vt, w_table, l, n_pool, True, 1.0)
        qm_diff = (dqb.astype(F32) * (1.0 / LOG2E))[:, None, :] * mask8[None]
        yd = _decode_attn(page_table, qm_diff, dk, dv, zero_col, slope_col, lw["diff_coef"], lw["diff_g"], g64,
                          diff_kt, diff_vt, w_table, l, n_pool, False, 1.0)
        x = _merge(x, mod_s[l], (ya, yb, yc, yd), lw, db, 1)
        x = _ffn(x, mod_s[l], lw, db, 1)
        s_conv.append(jnp.concatenate([conv_t[l][1:], glu[None]], axis=0).transpose(1, 0, 2))
        s_hgrn.append(s_new)
        s_logf.append(clft[:N_HEADS].T.reshape(db, 1, N_HEADS))
    y_sample = x.reshape(db, 1, d)
    sample_layout = lambda a: cache_layout(a).transpose(0, 2, 1, 3, 4)
    s_out = [jnp.stack(s_conv), jnp.stack(s_hgrn), sample_layout(kv_s[0]), sample_layout(kv_s[1]),
             jnp.stack(s_logf), sample_layout(kv_s[2]), sample_layout(kv_s[3])]
    return (y_prompt, y_sample, *p_out, *s_out)
```

```python
import functools
import math

import numpy as np
import jax
import jax.numpy as jnp
from jax import lax
from jax.experimental import pallas as pl
from jax.experimental.pallas import tpu as pltpu

F32 = jnp.float32
BF16 = jnp.bfloat16

EPS = 1e-6
NEG = -1e30
LOG2E = math.log2(math.e)
LB_FLOOR = 1e-30
CONV_W = 31
N_HEADS = 4
HGRN_CHUNK = 64
CONV_HALO = 32
ROWS = 8
LANE = 128
SUBLANES = 8
V7X_VMEM_LIMIT = 56 * 2**20


def _params(sem, vmem=V7X_VMEM_LIMIT):
    return pltpu.CompilerParams(dimension_semantics=sem, vmem_limit_bytes=vmem)


def _const_spec(shape):
    n = len(shape)
    return pl.BlockSpec(shape, lambda *_: (0,) * n)


def _dot(a, b):
    return jnp.dot(a, b, preferred_element_type=F32)


def _dot_nt(a, b):
    return lax.dot_general(a, b, (((1,), (1,)), ((), ())), preferred_element_type=F32)


def _dot_tn(a, b):
    return lax.dot_general(a, b, (((0,), (0,)), ((), ())), preferred_element_type=F32)


def _split3(a):
    hi = a.astype(BF16)
    r = a - hi.astype(F32)
    mid = r.astype(BF16)
    lo = (r - mid.astype(F32)).astype(BF16)
    return hi, mid, lo


def _dot3(a, m):
    hi, mid, lo = _split3(a)
    return _dot(hi, m) + _dot(mid, m) + _dot(lo, m)


def _dot3_left(m, a):
    hi, mid, lo = _split3(a)
    return _dot(m, hi) + _dot(m, mid) + _dot(m, lo)


def _sigmoid(x):
    return jax.nn.sigmoid(x)


def _silu(x):
    return x * jax.nn.sigmoid(x)


def _log_sigmoid(x):
    return jnp.minimum(x, 0.0) - jnp.log1p(jnp.exp(-jnp.abs(x)))


def _mod_norm(x, g, sc, sh):
    ms = jnp.mean(x * x, axis=-1, keepdims=True)
    return (x * lax.rsqrt(ms + EPS) * g) * (1.0 + sc) + sh


def _group_norm(z, ones_ref, n, gain):
    ss = _dot((z * z).astype(BF16), ones_ref[...]) * (1.0 / n)
    return z * lax.rsqrt(ss + EPS) * gain


def _ada_kernel(c_ref, w_ref, b_ref, o_ref):
    a = _silu(c_ref[...]).astype(BF16)
    o_ref[...] = _dot(a, w_ref[...].astype(BF16)) + b_ref[...]


def _ada(c_all, w_ada, b_ada):
    depth, d, n = w_ada.shape
    rows = c_all.shape[0]
    tn = d
    return pl.pallas_call(
        _ada_kernel,
        grid=(depth, n // tn),
        in_specs=[pl.BlockSpec((rows, d), lambda l, j: (0, 0)),
                  pl.BlockSpec((None, d, tn), lambda l, j: (l, 0, j)),
                  pl.BlockSpec((None, 1, tn), lambda l, j: (l, 0, j))],
        out_specs=pl.BlockSpec((None, rows, tn), lambda l, j: (l, 0, j)),
        out_shape=jax.ShapeDtypeStruct((depth, rows, n), F32),
        compiler_params=_params(("arbitrary", "arbitrary")),
        name="ada_mod",
    )(c_all, w_ada, b_ada.reshape(depth, 1, n))


W_PREP_ROWS = 256


def _wprep_kernel(w_ref, o_ref):
    o_ref[...] = w_ref[...].T.astype(BF16)


def _weight_prep(wt, row0, n_rows):
    d = wt.shape[1]
    bn = W_PREP_ROWS
    assert row0 % bn == 0 and n_rows % bn == 0
    return pl.pallas_call(
        _wprep_kernel,
        grid=(n_rows // bn,),
        in_specs=[pl.BlockSpec((bn, d), lambda i: (row0 // bn + i, 0))],
        out_specs=pl.BlockSpec((d, bn), lambda i: (0, i)),
        out_shape=jax.ShapeDtypeStruct((d, n_rows), BF16),
        compiler_params=_params(("arbitrary",)),
        name="weight_prep",
    )(wt)


N_INPROJ_IN = 16
N_INPROJ_STATE0 = 8


def _inproj_kernel(*refs, w, n_alias):
    (x_ref, sh_ref, sc_ref, g1_ref, wabc_ref, wd_ref, wft_ref, bfc_ref,
     oml_ref, dlb_ref, fqn_ref, fkn_ref, dqn_ref, dkn_ref, g64_ref, g32_ref) = refs[:N_INPROJ_IN]
    (glu_o, bq_o, bk_o, bv_o, blf_o, bgs_o, ckb_o, dkb_o,
     ckt_o, cvt_o, dkt_o, dvt_o, cqtb_o, cvtb_o, dqtb_o, dvtb_o, clft_o) = refs[N_INPROJ_IN + n_alias:]
    h = _mod_norm(x_ref[...], g1_ref[...], sc_ref[...], sh_ref[...]).astype(BF16)

    def seg(j):
        if j < 9:
            return _dot(h, wabc_ref[:, j * w:(j + 1) * w])
        return _dot(h, wd_ref[:, (j - 9) * w:(j - 8) * w])

    z = [seg(j) for j in range(12)]
    glu_o[...] = z[0] * _sigmoid(z[1])
    bq_o[...] = _silu(z[2])
    key = oml_ref[...] * (1.0 / (1.0 + jnp.exp(z[3]))) - dlb_ref[...]
    bk_o[...] = key
    blf_o[...] = jnp.log1p(-key)
    bv_o[...] = z[4]
    bgs_o[...] = _silu(z[5])
    cq = _group_norm(z[6], g64_ref, 64, fqn_ref[...])
    cqtb_o[...] = (cq * (64 ** -0.5 * LOG2E)).T.astype(BF16)
    ck = _group_norm(z[7], g64_ref, 64, fkn_ref[...])
    ckb_o[...] = ck.astype(BF16)
    ckt_o[...] = ck.T
    cvt = z[8].T
    cvt_o[...] = cvt
    cvtb_o[...] = cvt.astype(BF16)
    clft_o[...] = _log_sigmoid(_dot_nt(wft_ref[...], h) + bfc_ref[...])
    dq = _group_norm(z[9], g32_ref, 32, dqn_ref[...])
    dqtb_o[...] = (dq * (32 ** -0.5 * LOG2E)).T.astype(BF16)
    dk = _group_norm(z[10], g32_ref, 32, dkn_ref[...])
    dkb_o[...] = dk.astype(BF16)
    dkt_o[...] = dk.T
    dvt = z[11].T
    dvt_o[...] = dvt
    dvtb_o[...] = dvt.astype(BF16)


def _inproj(x, mod, lw, tm, batch, layer, depth, prev_states):
    t, d = x.shape
    w = d // 4
    s = t // batch
    tpb = s // tm
    r = mod.shape[1]
    row = lambda i: (i, 0)
    f32w = jax.ShapeDtypeStruct((t, w), F32)
    bf16w = jax.ShapeDtypeStruct((t, w), BF16)
    state = jax.ShapeDtypeStruct((depth, batch, w, s), F32)
    wspec = pl.BlockSpec((tm, w), row)
    vec = _const_spec((1, w))
    state_spec = pl.BlockSpec((None, None, w, tm), lambda i: (layer, i // tpb, 0, i % tpb))
    tspec = pl.BlockSpec((None, w, tm), lambda i: (i // tpb, 0, i % tpb))
    out_shapes = ([f32w] * 6 + [bf16w] * 2 + [state] * 4 + [jax.ShapeDtypeStruct((batch, w, s), BF16)] * 4
                  + [jax.ShapeDtypeStruct((ROWS, t), F32)])
    out_specs = [wspec] * 8 + [state_spec] * 4 + [tspec] * 4 + [pl.BlockSpec((ROWS, tm), lambda i: (0, i))]
    in_specs = [pl.BlockSpec((tm, d), row),
                pl.BlockSpec((None, r, d), lambda i: (i // tpb, 0, 0)),
                pl.BlockSpec((None, r, d), lambda i: (i // tpb, 0, 1)),
                _const_spec((1, d)),
                _const_spec((d, 9 * w)), _const_spec((d, 3 * w)),
                _const_spec((ROWS, d)), _const_spec((ROWS, 1)),
                vec, vec, vec, vec, vec, vec, _const_spec((w, w)), _const_spec((w, w))]
    args = [x, mod, mod, lw["g1"], lw["w_abc"], lw["w_d"], lw["w_ft"], lw["bf_col"],
            lw["one_minus_lb"], lw["dlb"], lw["fqn"], lw["fkn"], lw["dqn"], lw["dkn"], lw["g64"], lw["g32"]]
    assert len(args) == N_INPROJ_IN
    aliases = {}
    for n, a in enumerate(prev_states):
        aliases[len(args)] = N_INPROJ_STATE0 + n
        args.append(a)
        in_specs.append(pl.BlockSpec(memory_space=pl.ANY))
    return pl.pallas_call(
        functools.partial(_inproj_kernel, w=w, n_alias=len(aliases)),
        grid=(t // tm,),
        in_specs=in_specs,
        out_specs=out_specs,
        out_shape=out_shapes,
        input_output_aliases=aliases,
        compiler_params=_params(("arbitrary",)),
        name="in_proj",
    )(*args)


def _ln_silu(acc, g, b):
    mu = jnp.mean(acc, axis=-1, keepdims=True)
    cen = acc - mu
    var = jnp.mean(cen * cen, axis=-1, keepdims=True)
    return _silu(cen * lax.rsqrt(var + EPS) * g + b)


def _conv_kernel(cur_ref, prev_ref, w_ref, b_ref, g_ref, beta_ref, o_ref, win, *, tc, sub):
    i = pl.program_id(1)
    halo = prev_ref[tc - CONV_HALO:tc, :]
    win[0:CONV_HALO, :] = jnp.where(i > 0, halo, 0.0)
    win[CONV_HALO:CONV_HALO + tc, :] = cur_ref[...]
    off = CONV_HALO - (CONV_W - 1)
    for r0 in range(0, tc, sub):
        acc = jnp.broadcast_to(b_ref[...], (sub, b_ref.shape[1]))
        for j in range(CONV_W):
            acc = acc + win[r0 + off + j:r0 + off + j + sub, :] * w_ref[j:j + 1, :]
        o_ref[r0:r0 + sub, :] = _ln_silu(acc, g_ref[...], beta_ref[...]).astype(o_ref.dtype)


def _conv_prompt(glu, lw, batch, tc):
    t, w = glu.shape
    npb = t // batch // tc
    vec = _const_spec((1, w))
    return pl.pallas_call(
        functools.partial(_conv_kernel, tc=tc, sub=min(64, tc)),
        grid=(batch, npb),
        in_specs=[pl.BlockSpec((tc, w), lambda b, i: (b * npb + i, 0)),
                  pl.BlockSpec((tc, w), lambda b, i: (b * npb + jnp.maximum(i - 1, 0), 0)),
                  _const_spec((CONV_HALO, w)), vec, vec, vec],
        out_specs=pl.BlockSpec((tc, w), lambda b, i: (b * npb + i, 0)),
        out_shape=jax.ShapeDtypeStruct((t, w), BF16),
        scratch_shapes=[pltpu.VMEM((tc + CONV_HALO, w), F32)],
        compiler_params=_params(("arbitrary", "arbitrary")),
        name="conv_prompt",
    )(glu, glu, lw["conv_w"], lw["conv_b"], lw["conv_ln_g"], lw["conv_ln_b"])


def _hgrn_kernel(q_ref, k_ref, v_ref, lf_ref, gs_ref, gain_ref, tri_ref, bd_ref,
                 y_ref, st_ref, st, e_scr, *, c, w):
    i = pl.program_id(1)

    @pl.when(i == 0)
    def _():
        st[...] = jnp.zeros_like(st)

    q = q_ref[...]
    k = k_ref[...]
    v = v_ref[...]
    bd = bd_ref[...]
    g = _dot3_left(tri_ref[...], lf_ref[...])
    g_last = g[c - 1:c, :]
    o = _dot_nt((q * jnp.exp(g)).astype(BF16), st[...].astype(BF16))
    row = lax.broadcasted_iota(jnp.int32, (8, w), 0)
    offs = []
    pos = 0
    for s in range(c):
        s8 = (s // 8) * 8
        n = c - s8
        d = g[s8:, :] - g[s:s + 1, :]
        qk = q[s8:, :] * k[s:s + 1, :]
        head = jnp.where(row >= s - s8, d[0:8, :], NEG)
        d = head if n == 8 else jnp.concatenate([head, d[8:, :]], axis=0)
        e_scr[pos:pos + n, :] = (jnp.exp(d) * qk).astype(BF16)
        offs.append((pos, s8, n))
        pos += n
    red = _dot(e_scr[...], bd)
    parts = []
    for blk in range(c // 8):
        acc = jnp.zeros((8, w), F32)
        for s in range(min(c, blk * 8 + 8)):
            p0, s8, n = offs[s]
            acc = acc + red[p0 + blk * 8 - s8:p0 + blk * 8 - s8 + 8, :] * v[s:s + 1, :]
        parts.append(acc)
    o = o + jnp.concatenate(parts, axis=0)
    kv = _dot_tn(v.astype(BF16), (k * jnp.exp(g_last - g)).astype(BF16))
    st[...] = st[...] * jnp.exp(g_last) + kv * bd.astype(F32)
    y_ref[...] = (_group_norm(o, bd_ref, 64, gain_ref[...]) * gs_ref[...]).astype(y_ref.dtype)

    @pl.when(i == pl.num_programs(1) - 1)
    def _():
        st_ref[...] = st[...]


def _hgrn_rows(c):
    return sum(c - (s // 8) * 8 for s in range(c))


def _hgrn_prompt(bq, bk, bv, blf, bgs, lw, batch):
    t, w = bq.shape
    c = HGRN_CHUNK
    nc = t // batch // c
    blk = pl.BlockSpec((c, w), lambda b, i: (b * nc + i, 0))
    return pl.pallas_call(
        functools.partial(_hgrn_kernel, c=c, w=w),
        grid=(batch, nc),
        in_specs=[blk, blk, blk, blk, blk, _const_spec((1, w)), _const_spec((c, c)), _const_spec((w, w))],
        out_specs=[blk, pl.BlockSpec((None, w, w), lambda b, i: (b, 0, 0))],
        out_shape=[jax.ShapeDtypeStruct((t, w), BF16), jax.ShapeDtypeStruct((batch, w, w), F32)],
        scratch_shapes=[pltpu.VMEM((w, w), F32), pltpu.VMEM((_hgrn_rows(c), w), BF16)],
        compiler_params=_params(("arbitrary", "arbitrary")),
        name="hgrn_prompt",
    )(bq, bk, bv, blf, bgs, lw["hgrn_g"], lw["tri_hgrn"], lw["g64"])


def _cum_kernel(lf_ref, tri_ref, o_ref, carry):
    i = pl.program_id(1)

    @pl.when(i == 0)
    def _():
        carry[...] = jnp.zeros_like(carry)

    lf = lf_ref[...]
    lf2 = jnp.concatenate([lf, lf], axis=0)
    cum = _dot3(lf2, tri_ref[...])[0:ROWS, :] + carry[:, 0:1]
    carry[...] = jnp.broadcast_to(cum[:, cum.shape[1] - 1:], carry.shape)
    pad = jnp.zeros((LANE - ROWS, cum.shape[1]), F32)
    o_ref[...] = jnp.concatenate([-cum, pad], axis=0).T


def _neg_cumsum(lft, tri, batch, tcs):
    rows, t = lft.shape
    npb = t // batch // tcs
    return pl.pallas_call(
        _cum_kernel,
        grid=(batch, npb),
        in_specs=[pl.BlockSpec((rows, tcs), lambda b, i: (0, b * npb + i)), _const_spec((tcs, tcs))],
        out_specs=pl.BlockSpec((tcs, LANE), lambda b, i: (b * npb + i, 0)),
        out_shape=jax.ShapeDtypeStruct((t, LANE), F32),
        scratch_shapes=[pltpu.VMEM((rows, LANE), F32)],
        compiler_params=_params(("arbitrary", "arbitrary")),
        name="fox_cumsum",
    )(lft, tri)


ATTN_QW = 256
ATTN_KC = 128
ATTN_BUFS = 4


def _attn_kernel(qi_ref, ki_ref, qt_ref, k_ref, vt_ref, bq_ref, bk_ref, lam_ref, gain_ref, bd_ref,
                 o_ref, qtm, m_scr, l_scr, acc_scr, bias_scr, s_scr, p_scr, *, r_rows, diff, slopes, t, w):
    j = pl.program_id(1)
    qi = qi_ref[j]
    ki = ki_ref[j]
    dh = w // N_HEADS
    width = w // r_rows
    qw, kc_len = min(ATTN_QW, t), min(ATTN_KC, t)
    key_iota = lax.broadcasted_iota(jnp.int32, (kc_len, qw), 0)

    @pl.when(ki == 0)
    def _():
        qt = qt_ref[...]
        owner = lax.broadcasted_iota(jnp.int32, (w, t), 0) // width
        for r in range(r_rows):
            qtm[r] = jnp.where(owner == r, qt, jnp.zeros_like(qt))
        m_scr[...] = jnp.full_like(m_scr, NEG)
        l_scr[...] = jnp.zeros_like(l_scr)
        acc_scr[...] = jnp.zeros_like(acc_scr)
        if diff:
            for h in range(N_HEADS):
                bias_scr[h] = (slopes[h] * LOG2E) * key_iota.astype(F32)

    def step(masked):
        if diff:
            tile_off = ((ki - qi) * t).astype(F32)
        else:
            for h in range(N_HEADS):
                col = (bk_ref[:, h:h + 1] - bq_ref[0:1, h:h + 1]) * LOG2E
                bias_scr[h] = jnp.broadcast_to(col, (t, LANE))

        def logits(g, r, q0):
            h = r * N_HEADS // r_rows
            n_keys = min(t, q0 + qw) if masked else t
            chunks = range(0, n_keys, kc_len)
            shifts = [slopes[h] * LOG2E * (tile_off + float(k0)) if diff else 0.0 for k0 in chunks]
            m_run = m_scr[r, :, q0:q0 + qw]
            m8 = jnp.full((SUBLANES, qw), NEG, F32)
            for k0, shift in zip(chunks, shifts):
                s = _dot(k_ref[k0:k0 + kc_len, :], qtm[r, :, q0:q0 + qw])
                if diff:
                    s = s + bias_scr[h]
                else:
                    s = s + jnp.concatenate([bias_scr[h, k0:k0 + kc_len, :]] * (qw // LANE), axis=1)
                if masked and k0 + kc_len - 1 > q0:
                    q_iota = lax.broadcasted_iota(jnp.int32, (kc_len, qw), 1)
                    s = jnp.where(key_iota + (k0 - q0) <= q_iota, s, NEG)
                s_scr[g % ATTN_BUFS, k0:k0 + kc_len, :] = s
                m8 = jnp.maximum(m8, jnp.max(s.reshape(kc_len // SUBLANES, SUBLANES, qw), axis=0) + shift)
            m_new = jnp.maximum(m_run, jnp.max(m8, axis=0, keepdims=True))
            return dict(buf=g % ATTN_BUFS, r=r, h=h, q0=q0, n_keys=n_keys, chunks=chunks, shifts=shifts,
                        m_new=m_new, alpha=jnp.exp2(m_run - m_new))

        def weigh(st):
            buf, r, h, q0, m_new, alpha = st["buf"], st["r"], st["h"], st["q0"], st["m_new"], st["alpha"]
            l8 = jnp.zeros((SUBLANES, qw), F32)
            for k0, shift in zip(st["chunks"], st["shifts"]):
                p = jnp.exp2(s_scr[buf, k0:k0 + kc_len, :] - (m_new - shift))
                l8 = l8 + jnp.sum(p.reshape(kc_len // SUBLANES, SUBLANES, qw), axis=0)
                p_scr[buf, k0:k0 + kc_len, :] = p.astype(BF16)
            l_add = jnp.sum(l8, axis=0, keepdims=True)
            pv = _dot(vt_ref[h * dh:(h + 1) * dh, 0:st["n_keys"]], p_scr[buf, 0:st["n_keys"], :])
            m_scr[r, :, q0:q0 + qw] = m_new
            l_scr[r, :, q0:q0 + qw] = alpha * l_scr[r, :, q0:q0 + qw] + l_add
            acc_scr[r, :, q0:q0 + qw] = alpha * acc_scr[r, :, q0:q0 + qw] + pv

        groups = [(r, q0) for r in range(r_rows) for q0 in range(0, t, qw)]
        ahead = ATTN_BUFS - 1
        pending = [logits(g, *groups[g]) for g in range(min(ahead, len(groups)))]
        for g in range(len(groups)):
            if g + ahead < len(groups):
                pending.append(logits(g + ahead, *groups[g + ahead]))
            weigh(pending.pop(0))

    @pl.when(ki < qi)
    def _():
        step(False)

    @pl.when(ki == qi)
    def _():
        step(True)
        heads = []
        for h in range(N_HEADS):
            if diff:
                o_h = (acc_scr[2 * h] * (1.0 / l_scr[2 * h])
                       - lam_ref[...] * (acc_scr[2 * h + 1] * (1.0 / l_scr[2 * h + 1])))
            else:
                o_h = acc_scr[h] * (1.0 / l_scr[h])
            heads.append(o_h)
        out = jnp.concatenate(heads, axis=0).T
        if diff:
            out = _group_norm(out, bd_ref, dh, gain_ref[...])
        o_ref[...] = out.astype(o_ref.dtype)


def _attn_prompt(qt, k, vt, bias, lam, gain, g64, batch, t_blk, r_rows, diff, slopes):
    w = qt.shape[1]
    t = k.shape[0]
    nq = t // batch // t_blk
    pairs = [(a, b) for a in range(nq) for b in range(a + 1)]
    qi = jnp.asarray([p[0] for p in pairs], jnp.int32)
    ki = jnp.asarray([p[1] for p in pairs], jnp.int32)
    oblk = pl.BlockSpec((t_blk, w), lambda b, j, qi, ki: (b * nq + qi[j], 0))
    kblk = pl.BlockSpec((t_blk, w), lambda b, j, qi, ki: (b * nq + ki[j], 0))
    qtblk = pl.BlockSpec((None, w, t_blk), lambda b, j, qi, ki: (b, 0, qi[j]))
    vtblk = pl.BlockSpec((None, w, t_blk), lambda b, j, qi, ki: (b, 0, ki[j]))
    if diff:
        bias = jnp.zeros((ROWS, LANE), F32)
        bias_specs = [_const_spec((ROWS, LANE))] * 2
        bias_scr = pltpu.VMEM((N_HEADS, min(ATTN_KC, t_blk), min(ATTN_QW, t_blk)), F32)
    else:
        bias_specs = [pl.BlockSpec((t_blk, LANE), lambda b, j, qi, ki: (b * nq + qi[j], 0)),
                      pl.BlockSpec((t_blk, LANE), lambda b, j, qi, ki: (b * nq + ki[j], 0))]
        bias_scr = pltpu.VMEM((N_HEADS, t_blk, LANE), F32)
    grid_spec = pltpu.PrefetchScalarGridSpec(
        num_scalar_prefetch=2,
        grid=(batch, len(pairs)),
        in_specs=[qtblk, kblk, vtblk, *bias_specs,
                  _const_spec((1, 1)), _const_spec((1, w)), _const_spec((w, w))],
        out_specs=oblk,
        scratch_shapes=[pltpu.VMEM((r_rows, w, t_blk), BF16),
                        pltpu.VMEM((r_rows, 1, t_blk), F32),
                        pltpu.VMEM((r_rows, 1, t_blk), F32),
                        pltpu.VMEM((r_rows, w // N_HEADS, t_blk), F32),
                        bias_scr,
                        pltpu.VMEM((ATTN_BUFS, t_blk, min(ATTN_QW, t_blk)), F32),
                        pltpu.VMEM((ATTN_BUFS, t_blk, min(ATTN_QW, t_blk)), BF16)])
    return pl.pallas_call(
        functools.partial(_attn_kernel, r_rows=r_rows, diff=diff, slopes=slopes, t=t_blk, w=w),
        grid_spec=grid_spec,
        out_shape=jax.ShapeDtypeStruct((t, w), BF16),
        compiler_params=_params(("arbitrary", "arbitrary")),
        name="diff_prompt" if diff else "fox_prompt",
    )(qi, ki, qt, k, vt, bias, bias, lam, gain, g64)


def _merge_kernel(x_ref, sh_ref, sc_ref, gt_ref, g1_ref, ya_ref, yb_ref, yc_ref, yd_ref,
                  wg_ref, wb_ref, wo_ref, o_ref, *, d, w):
    x = x_ref[...]
    h = _mod_norm(x, g1_ref[...], sc_ref[...], sh_ref[...]).astype(BF16)
    merged = jnp.zeros(x.shape, F32)
    for j, y_ref in enumerate((ya_ref, yb_ref, yc_ref, yd_ref)):
        gate = _sigmoid(_dot(h, wg_ref[:, j * d:(j + 1) * d]))
        merged = merged + gate * _dot(y_ref[...].astype(BF16), wb_ref[j * w:(j + 1) * w, :])
    o_ref[...] = x + gt_ref[...] * _dot(merged.astype(BF16), wo_ref[...])


def _merge(x, mod, ys, lw, tm, tiles_per_mod):
    t, d = x.shape
    w = d // 4
    r = mod.shape[1]
    row = lambda i: (i, 0)
    modspec = lambda col: pl.BlockSpec((None, r, d), lambda i: (i // tiles_per_mod, 0, col))
    yspec = pl.BlockSpec((tm, w), row)
    return pl.pallas_call(
        functools.partial(_merge_kernel, d=d, w=w),
        grid=(t // tm,),
        in_specs=[pl.BlockSpec((tm, d), row), modspec(0), modspec(1), modspec(2), _const_spec((1, d)),
                  yspec, yspec, yspec, yspec,
                  _const_spec((d, 4 * d)), _const_spec((d, d)), _const_spec((d, d))],
        out_specs=pl.BlockSpec((tm, d), row),
        out_shape=jax.ShapeDtypeStruct((t, d), F32),
        compiler_params=_params(("arbitrary",)),
        name="merge_out",
    )(x, mod, mod, mod, lw["g1"], *ys, lw["w_gates"], lw["w_branch"], lw["w_out"])


def _ffn_kernel(x_ref, sh_ref, sc_ref, gt_ref, g2_ref, w1_ref, w2_ref, o_ref, *, d, n_chunks):
    x = x_ref[...]
    h = _mod_norm(x, g2_ref[...], sc_ref[...], sh_ref[...]).astype(BF16)
    acc = jnp.zeros(x.shape, F32)
    for cidx in range(n_chunks):
        u = jnp.maximum(_dot(h, w1_ref[:, cidx * d:(cidx + 1) * d]), 0.0)
        acc = acc + _dot((u * u).astype(BF16), w2_ref[cidx * d:(cidx + 1) * d, :])
    o_ref[...] = x + gt_ref[...] * acc


def _ffn(x, mod, lw, tm, tiles_per_mod):
    t, d = x.shape
    dff = lw["w_ff1"].shape[1]
    r = mod.shape[1]
    row = lambda i: (i, 0)
    modspec = lambda col: pl.BlockSpec((None, r, d), lambda i: (i // tiles_per_mod, 0, col))
    return pl.pallas_call(
        functools.partial(_ffn_kernel, d=d, n_chunks=dff // d),
        grid=(t // tm,),
        in_specs=[pl.BlockSpec((tm, d), row), modspec(3), modspec(4), modspec(5), _const_spec((1, d)),
                  _const_spec((d, dff)), _const_spec((dff, d))],
        out_specs=pl.BlockSpec((tm, d), row),
        out_shape=jax.ShapeDtypeStruct((t, d), F32),
        compiler_params=_params(("arbitrary",)),
        name="ffn",
    )(x, mod, mod, mod, lw["g2"], lw["w_ff1"], lw["w_ff2"])


def _conv_step_kernel(state_ref, glu_ref, w_ref, b_ref, g_ref, beta_ref, o_ref):
    n = CONV_W - 1
    acc = glu_ref[...] * w_ref[n:n + 1, :] + b_ref[...]
    for j in range(n):
        acc = acc + state_ref[j] * w_ref[j:j + 1, :]
    o_ref[...] = _ln_silu(acc, g_ref[...], beta_ref[...])


def _conv_step(state_t, glu, lw):
    db, w = glu.shape
    return pl.pallas_call(
        _conv_step_kernel,
        out_shape=jax.ShapeDtypeStruct((db, w), F32),
        compiler_params=_params(None),
        name="conv_step",
    )(state_t, glu, lw["conv_w"], lw["conv_b"], lw["conv_ln_g"], lw["conv_ln_b"])


def _hgrn_step_kernel(s0_ref, qc_ref, kc_ref, lfc_ref, vr_ref, gsr_ref, gain_ref, y_ref, s_ref):
    s_new = jnp.exp(lfc_ref[...]) * s0_ref[...] + kc_ref[...] * vr_ref[...]
    s_ref[...] = s_new
    o = jnp.sum(qc_ref[...] * s_new, axis=2, keepdims=True)
    ms = jnp.mean(o * o, axis=-1, keepdims=True)
    y_ref[...] = o * lax.rsqrt(ms + EPS) * gain_ref[...] * gsr_ref[...]


def _hgrn_step(s0, bq, bk, bv, blf, bgs, gain):
    db, nh, dk, dv = s0.shape
    col = lambda a: a.reshape(db, nh, dk, 1)
    rowv = lambda a: a.reshape(db, nh, 1, dv)
    y, s = pl.pallas_call(
        _hgrn_step_kernel,
        out_shape=[jax.ShapeDtypeStruct((db, nh, 1, dv), F32), jax.ShapeDtypeStruct(s0.shape, F32)],
        compiler_params=_params(None),
        name="hgrn_step",
    )(s0, col(bq), col(bk), col(blf), rowv(bv), rowv(bgs), gain.reshape(1, 1, 1, dv))
    return y.reshape(db, nh * dv), s


def _suffix_kernel(lf_ref, m_ref, o_ref):
    o_ref[...] = _dot3(lf_ref[...], m_ref[...])


def _suffix_matrix(page):
    src = np.arange(page)[:, None]
    dst = np.arange(2 * page)[None, :]
    return jnp.asarray((dst >= page) | (src > dst), BF16)


def _suffix_table(logf_rows, page):
    n = logf_rows.shape[0]
    tr = math.gcd(n, 1024)
    out = pl.pallas_call(
        _suffix_kernel,
        grid=(n // tr,),
        in_specs=[pl.BlockSpec((tr, page), lambda i: (i, 0)), _const_spec((page, 2 * page))],
        out_specs=pl.BlockSpec((tr, 2 * page), lambda i: (i, 0)),
        out_shape=jax.ShapeDtypeStruct((n, 2 * page), F32),
        compiler_params=_params(("arbitrary",)),
        name="fox_suffix_table",
    )(logf_rows, _suffix_matrix(page))
    return out.reshape(n // ROWS, ROWS, 2 * page)


def _decode_kernel(pt_ref, qm_ref, kn_ref, vn_ref, cnew_ref, slope_ref, lam_ref, gain_ref,
                   kc_ref, vc_ref, wt_ref, o_ref,
                   kbuf, vbuf, wbuf, sem, m_scr, l_scr, acc_scr, carry, p_scr, alpha_scr,
                   *, fox, scale, g_pages, n_groups, n_seq, page, nbuf, page_base, w):
    n_it = n_seq * n_groups
    past = n_groups * g_pages * page
    n_live = N_HEADS if fox else 2 * N_HEADS
    dh = w // N_HEADS
    lane0 = lax.broadcasted_iota(jnp.int32, (dh, page), 1) == 0

    def copies(it, slot):
        b = it // n_groups
        c = n_groups - 1 - it % n_groups
        out = []
        for g in range(g_pages):
            pg = page_base + pt_ref[b, c * g_pages + g]
            out.append(pltpu.make_async_copy(kc_ref.at[pg], kbuf.at[slot, g], sem.at[0, slot]))
            out.append(pltpu.make_async_copy(vc_ref.at[pg], vbuf.at[slot, g], sem.at[1, slot]))
            if fox:
                out.append(pltpu.make_async_copy(wt_ref.at[pg], wbuf.at[slot, g], sem.at[2, slot]))
        return out

    ahead = nbuf - 2

    def score(it):
        slot = it % nbuf
        b = it // n_groups
        c = n_groups - 1 - it % n_groups
        first = c == n_groups - 1
        qm = qm_ref[b]
        s = jnp.concatenate([_dot(qm, kbuf[slot, g]) for g in range(g_pages)], axis=1) * scale
        if fox:
            parts = [None] * g_pages
            run = jnp.where(first, 0.0, carry[...])
            for g in reversed(range(g_pages)):
                parts[g] = wbuf[slot, g, :, 0:page] + run
                run = run + wbuf[slot, g, :, page:2 * page]
            carry[...] = run
            s = s + jnp.concatenate(parts, axis=1) + cnew_ref[b]
        else:
            pos = (c * (g_pages * page) + lax.broadcasted_iota(jnp.int32, (1, g_pages * page), 1)).astype(F32)
            s = s + slope_ref[...] * (pos - float(past))
        m_prev = jnp.where(first, jnp.sum(qm * kn_ref[b], axis=-1, keepdims=True) * scale, m_scr[...])
        l_prev = jnp.where(first, 1.0, l_scr[...])
        m_new = jnp.maximum(m_prev, jnp.max(s, axis=-1, keepdims=True))
        alpha = jnp.exp(m_prev - m_new)
        p = jnp.exp(s - m_new)
        l_scr[...] = alpha * l_prev + jnp.sum(p, axis=-1, keepdims=True)
        m_scr[...] = m_new
        p_scr[...] = p
        alpha_scr[...] = alpha

    def weigh(jt):
        slot = jt % nbuf
        b = jt // n_groups
        c = n_groups - 1 - jt % n_groups
        first = c == n_groups - 1
        l_now = l_scr[...]
        for r in range(n_live):
            h = r * N_HEADS // n_live
            rows = slice(h * dh, (h + 1) * dh)
            start = jnp.where(lane0, vn_ref[b, rows, :], 0.0)
            acc = jnp.where(first, start, acc_scr[r]) * alpha_scr[r:r + 1, :]
            for g in range(g_pages):
                acc = acc + vbuf[slot, g, rows, :] * p_scr[r:r + 1, g * page:(g + 1) * page]
            acc_scr[r] = acc
        return l_now, b, c == 0

    def finish(l_now, b):
        cols = [jnp.sum(acc_scr[r], axis=1, keepdims=True) / l_now[r:r + 1, :] for r in range(n_live)]
        heads = []
        for h in range(N_HEADS):
            if fox:
                o_h = cols[h]
            else:
                o_h = cols[2 * h] - lam_ref[...] * cols[2 * h + 1]
                ms = jnp.mean(o_h * o_h, axis=0, keepdims=True)
                o_h = o_h * lax.rsqrt(ms + EPS) * gain_ref[h * dh:(h + 1) * dh, :]
            heads.append(o_h)
        o_ref[b] = jnp.concatenate(heads, axis=0)

    for it0 in range(min(ahead, n_it)):
        for cp in copies(it0, it0 % nbuf):
            cp.start()

    for ref in (p_scr, alpha_scr, acc_scr, carry, m_scr):
        ref[...] = jnp.zeros_like(ref)
    l_scr[...] = jnp.ones_like(l_scr)

    def body(it, _):
        @pl.when(it + ahead < n_it)
        def _():
            for cp in copies(it + ahead, (it + ahead) % nbuf):
                cp.start()

        @pl.when(it < n_it)
        def _():
            for cp in copies(it, it % nbuf):
                cp.wait()

        l_now, b, last = weigh(jnp.maximum(it - 1, 0))
        score(jnp.minimum(it, n_it - 1))

        @pl.when(jnp.logical_and(last, it > 0))
        def _():
            finish(l_now, b)

        return 0

    lax.fori_loop(0, n_it + 1, body, 0)


def _decode_attn(page_table, qm, k_new, v_new, cnew, slopes, lam, gain_col, kt_cache, vt_cache, w_table,
                 layer, n_pool, fox, scale):
    db, n_pages = page_table.shape
    w, page = kt_cache.shape[1], kt_cache.shape[2]
    g_pages = math.gcd(n_pages, 8)
    nbuf = 4
    n_groups = n_pages // g_pages
    full3 = lambda shape: pl.BlockSpec(shape, lambda i, pt: (0, 0, 0))
    full2 = lambda shape: pl.BlockSpec(shape, lambda i, pt: (0, 0))
    any_spec = pl.BlockSpec(memory_space=pl.ANY)
    grid_spec = pltpu.PrefetchScalarGridSpec(
        num_scalar_prefetch=1,
        grid=(1,),
        in_specs=[full3((db, ROWS, w)), full3((db, 1, w)), full3((db, w, 1)), full3((db, ROWS, 1)),
                  full2((ROWS, 1)), full2((1, 1)), full2((w, 1)),
                  any_spec, any_spec, any_spec],
        out_specs=full3((db, w, 1)),
        scratch_shapes=[pltpu.VMEM((nbuf, g_pages, w, page), F32),
                        pltpu.VMEM((nbuf, g_pages, w, page), F32),
                        pltpu.VMEM((nbuf, g_pages, ROWS, 2 * page), F32),
                        pltpu.SemaphoreType.DMA((3, nbuf)),
                        pltpu.VMEM((ROWS, 1), F32), pltpu.VMEM((ROWS, 1), F32),
                        pltpu.VMEM((ROWS, w // N_HEADS, page), F32), pltpu.VMEM((ROWS, page), F32),
                        pltpu.VMEM((ROWS, g_pages * page), F32), pltpu.VMEM((ROWS, 1), F32)])
    out = pl.pallas_call(
        functools.partial(_decode_kernel, fox=fox, scale=scale, g_pages=g_pages,
                          n_groups=n_groups, n_seq=db, page=page, nbuf=nbuf, page_base=layer * n_pool, w=w),
        grid_spec=grid_spec,
        out_shape=jax.ShapeDtypeStruct((db, w, 1), F32),
        compiler_params=_params(("arbitrary",)),
        name="fox_decode" if fox else "diff_decode",
    )(page_table, qm, k_new.reshape(db, 1, w), v_new.reshape(db, w, 1), cnew, slopes, lam, gain_col,
      kt_cache, vt_cache, w_table)
    return out.reshape(db, w)


def _block_diag_ones(w, n):
    idx = np.arange(w) // n
    return jnp.asarray(idx[:, None] == idx[None, :], BF16)


def _row_mask(w, r_rows):
    lane = np.arange(w) // (w // r_rows)
    return jnp.asarray(np.arange(ROWS)[:, None] == lane[None, :], F32)


def _alibi_slopes(n):
    return jnp.asarray(2.0 ** (-8.0 * np.arange(1, n + 1) / n), F32)


def _layer_weights(l, p, lbs, d):
    w = d // 4
    o_f = 9 * w
    o_d = o_f + N_HEADS
    w_in_t = p["w_in"].transpose(2, 0, 1)
    wt_abc = w_in_t[:o_f, l]
    wt_f = w_in_t[o_f:o_d, l]
    wt_rest = w_in_t[o_d:, l]
    lb = lbs[l]
    lbf = jnp.maximum(lb, LB_FLOOR)
    tile = lambda a, n: jnp.tile(a, n).reshape(1, w)
    conv_w = jnp.concatenate([p["conv_w"][l], jnp.zeros((CONV_HALO - CONV_W, w), F32)], axis=0)
    lam_init = 0.8 - 0.6 * math.exp(-0.3 * l)
    lam = (jnp.exp(jnp.sum(p["lam_q1"][l] * p["lam_k1"][l])) - jnp.exp(jnp.sum(p["lam_q2"][l] * p["lam_k2"][l]))
           + lam_init).astype(F32)
    return dict(
        g1=p["norm1_g"][l].reshape(1, d), g2=p["norm2_g"][l].reshape(1, d),
        w_abc=_weight_prep(wt_abc, 0, o_f), w_d=_weight_prep(wt_rest, 0, 3 * w),
        w_gates=_weight_prep(wt_rest, 3 * w, 4 * d),
        w_ft=jnp.pad(wt_f, ((0, ROWS - N_HEADS), (0, 0))).astype(BF16),
        bf_col=jnp.pad(p["b_fox_f"][l], (0, ROWS - N_HEADS)).reshape(ROWS, 1),
        one_minus_lb=(1.0 - lb).reshape(1, w), dlb=(lbf - lb).reshape(1, w),
        fqn=tile(p["fox_qn_g"][l], N_HEADS), fkn=tile(p["fox_kn_g"][l], N_HEADS),
        dqn=tile(p["diff_qn_g"][l], 2 * N_HEADS), dkn=tile(p["diff_kn_g"][l], 2 * N_HEADS),
        hgrn_g=tile(p["hgrn_norm_g"][l], N_HEADS), hgrn_g1=p["hgrn_norm_g"][l],
        diff_g=tile(p["diff_norm_g"][l], N_HEADS) * (1.0 - lam_init),
        conv_w=conv_w, conv_b=p["conv_b"][l].reshape(1, w),
        conv_ln_g=p["conv_ln_g"][l].reshape(1, w), conv_ln_b=p["conv_ln_b"][l].reshape(1, w),
        lam=lam.reshape(1, 1),
        w_branch=p["w_branch"][l].reshape(d, d).astype(BF16),
        w_out=p["w_out"][l].astype(BF16),
        w_ff1=p["w_ff1"][l].astype(BF16), w_ff2=p["w_ff2"][l].astype(BF16),
        ones_w=jnp.ones((1, w), F32),
    )


def _hgrn_state_from_transposed(st, w):
    b = st.shape[0]
    dh = w // N_HEADS
    st5 = st.reshape(b, N_HEADS, dh, N_HEADS, dh)
    return jnp.stack([st5[:, h, :, h, :] for h in range(N_HEADS)], axis=1).transpose(0, 1, 3, 2)


def kernel(x_prompt, x_sample, cache_fox_k, cache_fox_v, cache_fox_logf, cache_diff_k, cache_diff_v,
           state_conv, state_hgrn, page_table, c_prompt, c_sample, w_ada, b_ada, norm1_g, norm2_g, w_in,
           b_fox_f, lb_logits, hgrn_norm_g, conv_w, conv_b, conv_ln_g, conv_ln_b, fox_qn_g, fox_kn_g,
           diff_qn_g, diff_kn_g, lam_q1, lam_k1, lam_q2, lam_k2, diff_norm_g, w_branch, w_out, w_ff1, w_ff2):
    p = dict(norm1_g=norm1_g, norm2_g=norm2_g, w_in=w_in, b_fox_f=b_fox_f, hgrn_norm_g=hgrn_norm_g,
             conv_w=conv_w, conv_b=conv_b, conv_ln_g=conv_ln_g, conv_ln_b=conv_ln_b, fox_qn_g=fox_qn_g,
             fox_kn_g=fox_kn_g, diff_qn_g=diff_qn_g, diff_kn_g=diff_kn_g, lam_q1=lam_q1, lam_k1=lam_k1,
             lam_q2=lam_q2, lam_k2=lam_k2, diff_norm_g=diff_norm_g, w_branch=w_branch, w_out=w_out,
             w_ff1=w_ff1, w_ff2=w_ff2)
    batch, seq, d = x_prompt.shape
    db = x_sample.shape[0]
    assert x_sample.shape[1] == 1, "the sample group advances one token per step"
    depth = w_in.shape[0]
    w = d // 4
    dh = w // N_HEADS
    n_pool, page = cache_fox_k.shape[1], cache_fox_k.shape[2]
    n_pages = page_table.shape[1]

    lbs = jax.nn.softmax(lb_logits.astype(F32), axis=0)
    lbs = jnp.cumsum(lbs, axis=0) - lbs[0]
    lws = [_layer_weights(l, p, lbs, d) for l in range(depth)]
    g64 = _block_diag_ones(w, dh)
    g32 = _block_diag_ones(w, dh // 2)
    for lw in lws:
        lw["g64"], lw["g32"] = g64, g32

    rows = -(-(batch + db) // 8) * 8
    c_all = jnp.concatenate([c_prompt, c_sample, jnp.zeros((rows - batch - db, d), F32)], axis=0)
    mod = _ada(c_all, w_ada, b_ada)
    mod_p = mod[:, :batch].reshape(depth, batch, 1, 6 * d)
    mod_s = mod[:, batch:batch + db].reshape(depth, 1, db, 6 * d)

    t = batch * seq
    tm = math.gcd(seq, 512)
    t_attn = math.gcd(seq, 512)
    tcs = math.gcd(seq, 512)
    tri_cum = jnp.asarray(np.triu(np.ones((tcs, tcs))), BF16)
    tri_hgrn = jnp.asarray(np.tril(np.ones((HGRN_CHUNK, HGRN_CHUNK))), BF16)
    slope_vals = tuple(float(2.0 ** (-8.0 * (h + 1) / N_HEADS)) for h in range(N_HEADS))
    slopes = jnp.asarray(slope_vals, F32)

    def cache_layout(a):
        return a.reshape(a.shape[0], a.shape[1], N_HEADS, dh, a.shape[3]).transpose(0, 1, 4, 2, 3)

    x = x_prompt.reshape(t, d)
    kv_p = [jnp.zeros((depth, batch, w, seq), F32) for _ in range(4)]
    p_conv, p_hgrn, p_logf = [], [], []
    for l, lw in enumerate(lws):
        lw["tri_hgrn"] = tri_hgrn
        outs = _inproj(x, mod_p[l], lw, tm, batch, l, depth, kv_p)
        glu, bq, bk, bv, blf, bgs, ckb, dkb = outs[:8]
        kv_p = outs[8:12]
        cqtb, cvtb, dqtb, dvtb, clft = outs[12:]
        ya = _conv_prompt(glu, lw, batch, math.gcd(seq, 256))
        yb, st = _hgrn_prompt(bq, bk, bv, blf, bgs, lw, batch)
        fox_bias = _neg_cumsum(clft, tri_cum, batch, tcs)
        yc = _attn_prompt(cqtb, ckb, cvtb, fox_bias, lw["lam"], lw["ones_w"], g64, batch, t_attn,
                          N_HEADS, False, slope_vals)
        yd = _attn_prompt(dqtb, dkb, dvtb, None, lw["lam"], lw["diff_g"], g64, batch, t_attn,
                          2 * N_HEADS, True, slope_vals)
        x = _merge(x, mod_p[l], (ya, yb, yc, yd), lw, tm, seq // tm)
        x = _ffn(x, mod_p[l], lw, tm, seq // tm)
        p_conv.append(glu.reshape(batch, seq, w)[:, seq - (CONV_W - 1):])
        p_hgrn.append(_hgrn_state_from_transposed(st, w))
        p_logf.append(clft[:N_HEADS].reshape(N_HEADS, batch, seq).transpose(1, 2, 0))
    y_prompt = x.reshape(batch, seq, d)
    p_out = [jnp.stack(p_conv), jnp.stack(p_hgrn), cache_layout(kv_p[0]), cache_layout(kv_p[1]),
             jnp.stack(p_logf), cache_layout(kv_p[2]), cache_layout(kv_p[3])]

    pool_t = lambda a: a.transpose(0, 1, 3, 4, 2).reshape(depth * n_pool, w, page)
    fox_kt, fox_vt = pool_t(cache_fox_k), pool_t(cache_fox_v)
    diff_kt, diff_vt = pool_t(cache_diff_k), pool_t(cache_diff_v)
    logf_rows = jnp.pad(cache_fox_logf.transpose(0, 1, 3, 2), ((0, 0), (0, 0), (0, ROWS - N_HEADS), (0, 0)))
    w_table = _suffix_table(logf_rows.reshape(depth * n_pool * ROWS, page), page)
    mask4 = _row_mask(w, N_HEADS)
    mask8 = _row_mask(w, 2 * N_HEADS)
    slope_col = jnp.repeat(slopes, 2).reshape(ROWS, 1)
    zero_col = jnp.zeros((db, ROWS, 1), F32)
    head_rows = jnp.arange(ROWS)[:, None] < N_HEADS
    conv_t = state_conv.transpose(0, 2, 1, 3)

    x = x_sample.reshape(db, d)
    kv_s = [jnp.zeros((depth, 1, w, db), F32) for _ in range(4)]
    s_conv, s_hgrn, s_logf = [], [], []
    for l, lw in enumerate(lws):
        outs = _inproj(x, mod_s[l], lw, db, 1, l, depth, kv_s)
        glu, bq, bk, bv, blf, bgs = outs[:6]
        kv_s = outs[8:12]
        cqb, dqb = outs[12][0].T, outs[14][0].T
        clft = outs[16]
        ck, cv, dk, dv = [a[l, 0].T for a in kv_s]
        ya = _conv_step(conv_t[l], glu, lw)
        yb, s_new = _hgrn_step(state_hgrn[l], bq, bk, bv, blf, bgs, lw["hgrn_g1"])
        cnew = jnp.where(head_rows, clft, 0.0).T.reshape(db, ROWS, 1)
        qm_fox = (cqb.astype(F32) * (1.0 / LOG2E))[:, None, :] * mask4[None]
        yc = _decode_attn(page_table, qm_fox, ck, cv, cnew, slope_col, lw["lam"], lw["ones_w"].reshape(w, 1),
                          fox_kt, fox_vt, w_table, l, n_pool, True, 1.0)
        qm_diff = (dqb.astype(F32) * (1.0 / LOG2E))[:, None, :] * mask8[None]
        yd = _decode_attn(page_table, qm_diff, dk, dv, zero_col, slope_col, lw["lam"], lw["diff_g"].reshape(w, 1),
                          diff_kt, diff_vt, w_table, l, n_pool, False, 1.0)
        x = _merge(x, mod_s[l], (ya, yb, yc, yd), lw, db, 1)
        x = _ffn(x, mod_s[l], lw, db, 1)
        s_conv.append(jnp.concatenate([conv_t[l][1:], glu[None]], axis=0).transpose(1, 0, 2))
        s_hgrn.append(s_new)
        s_logf.append(clft[:N_HEADS].T.reshape(db, 1, N_HEADS))
    y_sample = x.reshape(db, 1, d)
    sample_layout = lambda a: cache_layout(a).transpose(0, 2, 1, 3, 4)
    s_out = [jnp.stack(s_conv), jnp.stack(s_hgrn), sample_layout(kv_s[0]), sample_layout(kv_s[1]),
             jnp.stack(s_logf), sample_layout(kv_s[2]), sample_layout(kv_s[3])]
    return (y_prompt, y_sample, *p_out, *s_out)
```

```python
import functools
import math

import numpy as np
import jax
import jax.numpy as jnp
from jax import lax
from jax.experimental import pallas as pl
from jax.experimental.pallas import tpu as pltpu

F32 = jnp.float32
BF16 = jnp.bfloat16

EPS = 1e-6
NEG = -1e30
LOG2E = math.log2(math.e)
LB_FLOOR = 1e-30
CONV_W = 31
N_HEADS = 4
HGRN_CHUNK = 64
HGRN_BLOCKS = 4
CONV_HALO = 32
ROWS = 8
LANE = 128
SUBLANES = 8
V7X_VMEM_LIMIT = 56 * 2**20


def _params(sem, vmem=V7X_VMEM_LIMIT):
    return pltpu.CompilerParams(dimension_semantics=sem, vmem_limit_bytes=vmem)


def _const_spec(shape):
    n = len(shape)
    return pl.BlockSpec(shape, lambda *_: (0,) * n)


def _dot(a, b):
    return jnp.dot(a, b, preferred_element_type=F32)


def _dot_nt(a, b):
    return lax.dot_general(a, b, (((1,), (1,)), ((), ())), preferred_element_type=F32)


def _dot_tn(a, b):
    return lax.dot_general(a, b, (((0,), (0,)), ((), ())), preferred_element_type=F32)


def _split3(a):
    hi = a.astype(BF16)
    r = a - hi.astype(F32)
    mid = r.astype(BF16)
    lo = (r - mid.astype(F32)).astype(BF16)
    return hi, mid, lo


def _dot3(a, m):
    hi, mid, lo = _split3(a)
    return _dot(hi, m) + _dot(mid, m) + _dot(lo, m)


def _dot3_left(m, a):
    hi, mid, lo = _split3(a)
    return _dot(m, hi) + _dot(m, mid) + _dot(m, lo)


def _sigmoid(x):
    return jax.nn.sigmoid(x)


def _silu(x):
    return x * jax.nn.sigmoid(x)


def _log_sigmoid(x):
    return jnp.minimum(x, 0.0) - jnp.log1p(jnp.exp(-jnp.abs(x)))


def _mod_norm(x, g, sc, sh):
    ms = jnp.mean(x * x, axis=-1, keepdims=True)
    return (x * lax.rsqrt(ms + EPS) * g) * (1.0 + sc) + sh


def _group_norm(z, ones_ref, n, gain):
    ss = _dot((z * z).astype(BF16), ones_ref[...]) * (1.0 / n)
    return z * lax.rsqrt(ss + EPS) * gain


def _ada_kernel(c_ref, w_ref, b_ref, o_ref):
    a = _silu(c_ref[...]).astype(BF16)
    o_ref[...] = _dot(a, w_ref[...].astype(BF16)) + b_ref[...]


def _ada(c_all, w_ada, b_ada):
    depth, d, n = w_ada.shape
    rows = c_all.shape[0]
    tn = d
    return pl.pallas_call(
        _ada_kernel,
        grid=(depth, n // tn),
        in_specs=[pl.BlockSpec((rows, d), lambda l, j: (0, 0)),
                  pl.BlockSpec((None, d, tn), lambda l, j: (l, 0, j)),
                  pl.BlockSpec((None, 1, tn), lambda l, j: (l, 0, j))],
        out_specs=pl.BlockSpec((None, rows, tn), lambda l, j: (l, 0, j)),
        out_shape=jax.ShapeDtypeStruct((depth, rows, n), F32),
        compiler_params=_params(("arbitrary", "arbitrary")),
        name="ada_mod",
    )(c_all, w_ada, b_ada.reshape(depth, 1, n))


W_PREP_ROWS = 256


def _wprep_kernel(w_ref, o_ref):
    o_ref[...] = w_ref[...].T.astype(BF16)


def _weight_prep(wt, row0, n_rows):
    d = wt.shape[1]
    bn = W_PREP_ROWS
    assert row0 % bn == 0 and n_rows % bn == 0
    return pl.pallas_call(
        _wprep_kernel,
        grid=(n_rows // bn,),
        in_specs=[pl.BlockSpec((bn, d), lambda i: (row0 // bn + i, 0))],
        out_specs=pl.BlockSpec((d, bn), lambda i: (0, i)),
        out_shape=jax.ShapeDtypeStruct((d, n_rows), BF16),
        compiler_params=_params(("arbitrary",)),
        name="weight_prep",
    )(wt)


N_INPROJ_IN = 16
N_INPROJ_STATE0 = 8


def _inproj_kernel(*refs, w, n_alias):
    (x_ref, sh_ref, sc_ref, g1_ref, wabc_ref, wd_ref, wft_ref, bfc_ref,
     oml_ref, dlb_ref, fqn_ref, fkn_ref, dqn_ref, dkn_ref, g64_ref, g32_ref) = refs[:N_INPROJ_IN]
    (glu_o, bq_o, bk_o, bv_o, blf_o, bgs_o, ckb_o, dkb_o,
     ckt_o, cvt_o, dkt_o, dvt_o, cqtb_o, cvtb_o, dqtb_o, dvtb_o, clft_o) = refs[N_INPROJ_IN + n_alias:]
    h = _mod_norm(x_ref[...], g1_ref[...], sc_ref[...], sh_ref[...]).astype(BF16)

    def seg(j):
        if j < 9:
            return _dot(h, wabc_ref[:, j * w:(j + 1) * w])
        return _dot(h, wd_ref[:, (j - 9) * w:(j - 8) * w])

    z = [seg(j) for j in range(12)]
    glu_o[...] = z[0] * _sigmoid(z[1])
    bq_o[...] = _silu(z[2])
    key = oml_ref[...] * (1.0 / (1.0 + jnp.exp(z[3]))) - dlb_ref[...]
    bk_o[...] = key
    blf_o[...] = jnp.log1p(-key)
    bv_o[...] = z[4]
    bgs_o[...] = _silu(z[5])
    cq = _group_norm(z[6], g64_ref, 64, fqn_ref[...])
    cqtb_o[...] = (cq * (64 ** -0.5 * LOG2E)).T.astype(BF16)
    ck = _group_norm(z[7], g64_ref, 64, fkn_ref[...])
    ckb_o[...] = ck.astype(BF16)
    ckt_o[...] = ck.T
    cvt = z[8].T
    cvt_o[...] = cvt
    cvtb_o[...] = cvt.astype(BF16)
    clft_o[...] = _log_sigmoid(_dot_nt(wft_ref[...], h) + bfc_ref[...])
    dq = _group_norm(z[9], g32_ref, 32, dqn_ref[...])
    dqtb_o[...] = (dq * (32 ** -0.5 * LOG2E)).T.astype(BF16)
    dk = _group_norm(z[10], g32_ref, 32, dkn_ref[...])
    dkb_o[...] = dk.astype(BF16)
    dkt_o[...] = dk.T
    dvt = z[11].T
    dvt_o[...] = dvt
    dvtb_o[...] = dvt.astype(BF16)


def _inproj(x, mod, lw, tm, batch, layer, depth, prev_states):
    t, d = x.shape
    w = d // 4
    s = t // batch
    tpb = s // tm
    r = mod.shape[1]
    row = lambda i: (i, 0)
    f32w = jax.ShapeDtypeStruct((t, w), F32)
    bf16w = jax.ShapeDtypeStruct((t, w), BF16)
    state = jax.ShapeDtypeStruct((depth, batch, w, s), F32)
    wspec = pl.BlockSpec((tm, w), row)
    vec = _const_spec((1, w))
    state_spec = pl.BlockSpec((None, None, w, tm), lambda i: (layer, i // tpb, 0, i % tpb))
    tspec = pl.BlockSpec((None, w, tm), lambda i: (i // tpb, 0, i % tpb))
    out_shapes = ([f32w] * 6 + [bf16w] * 2 + [state] * 4 + [jax.ShapeDtypeStruct((batch, w, s), BF16)] * 4
                  + [jax.ShapeDtypeStruct((ROWS, t), F32)])
    out_specs = [wspec] * 8 + [state_spec] * 4 + [tspec] * 4 + [pl.BlockSpec((ROWS, tm), lambda i: (0, i))]
    in_specs = [pl.BlockSpec((tm, d), row),
                pl.BlockSpec((None, r, d), lambda i: (i // tpb, 0, 0)),
                pl.BlockSpec((None, r, d), lambda i: (i // tpb, 0, 1)),
                _const_spec((1, d)),
                _const_spec((d, 9 * w)), _const_spec((d, 3 * w)),
                _const_spec((ROWS, d)), _const_spec((ROWS, 1)),
                vec, vec, vec, vec, vec, vec, _const_spec((w, w)), _const_spec((w, w))]
    args = [x, mod, mod, lw["g1"], lw["w_abc"], lw["w_d"], lw["w_ft"], lw["bf_col"],
            lw["one_minus_lb"], lw["dlb"], lw["fqn"], lw["fkn"], lw["dqn"], lw["dkn"], lw["g64"], lw["g32"]]
    assert len(args) == N_INPROJ_IN
    aliases = {}
    for n, a in enumerate(prev_states):
        aliases[len(args)] = N_INPROJ_STATE0 + n
        args.append(a)
        in_specs.append(pl.BlockSpec(memory_space=pl.ANY))
    return pl.pallas_call(
        functools.partial(_inproj_kernel, w=w, n_alias=len(aliases)),
        grid=(t // tm,),
        in_specs=in_specs,
        out_specs=out_specs,
        out_shape=out_shapes,
        input_output_aliases=aliases,
        compiler_params=_params(("arbitrary",)),
        name="in_proj",
    )(*args)


def _ln_silu(acc, g, b):
    mu = jnp.mean(acc, axis=-1, keepdims=True)
    cen = acc - mu
    var = jnp.mean(cen * cen, axis=-1, keepdims=True)
    return _silu(cen * lax.rsqrt(var + EPS) * g + b)


def _conv_kernel(cur_ref, prev_ref, w_ref, b_ref, g_ref, beta_ref, o_ref, win, *, tc, sub):
    i = pl.program_id(1)
    halo = prev_ref[tc - CONV_HALO:tc, :]
    win[0:CONV_HALO, :] = jnp.where(i > 0, halo, 0.0)
    win[CONV_HALO:CONV_HALO + tc, :] = cur_ref[...]
    off = CONV_HALO - (CONV_W - 1)
    for r0 in range(0, tc, sub):
        acc = jnp.broadcast_to(b_ref[...], (sub, b_ref.shape[1]))
        for j in range(CONV_W):
            acc = acc + win[r0 + off + j:r0 + off + j + sub, :] * w_ref[j:j + 1, :]
        o_ref[r0:r0 + sub, :] = _ln_silu(acc, g_ref[...], beta_ref[...]).astype(o_ref.dtype)


def _conv_prompt(glu, lw, batch, tc):
    t, w = glu.shape
    npb = t // batch // tc
    vec = _const_spec((1, w))
    return pl.pallas_call(
        functools.partial(_conv_kernel, tc=tc, sub=min(64, tc)),
        grid=(batch, npb),
        in_specs=[pl.BlockSpec((tc, w), lambda b, i: (b * npb + i, 0)),
                  pl.BlockSpec((tc, w), lambda b, i: (b * npb + jnp.maximum(i - 1, 0), 0)),
                  _const_spec((CONV_HALO, w)), vec, vec, vec],
        out_specs=pl.BlockSpec((tc, w), lambda b, i: (b * npb + i, 0)),
        out_shape=jax.ShapeDtypeStruct((t, w), BF16),
        scratch_shapes=[pltpu.VMEM((tc + CONV_HALO, w), F32)],
        compiler_params=_params(("arbitrary", "arbitrary")),
        name="conv_prompt",
    )(glu, glu, lw["conv_w"], lw["conv_b"], lw["conv_ln_g"], lw["conv_ln_b"])


def _hgrn_kernel(q_ref, k_ref, v_ref, lf_ref, gs_ref, gain_ref, tri_ref, bd_ref,
                 y_ref, st_ref, st, e_scr, *, c, nb, w):
    i = pl.program_id(1)

    @pl.when(i == 0)
    def _():
        st[...] = jnp.zeros_like(st)

    bd = bd_ref[...]
    row = lax.broadcasted_iota(jnp.int32, (SUBLANES, w), 0)
    offs = []
    pos = 0
    for s in range(c):
        s8 = (s // SUBLANES) * SUBLANES
        offs.append((pos, s8, c - s8))
        pos += c - s8

    def build(j):
        r0 = j * c
        q = q_ref[r0:r0 + c, :]
        k = k_ref[r0:r0 + c, :]
        g = _dot3_left(tri_ref[...], lf_ref[r0:r0 + c, :])
        for s in range(c):
            p0, s8, n = offs[s]
            d = g[s8:, :] - g[s:s + 1, :]
            qk = q[s8:, :] * k[s:s + 1, :]
            head = jnp.where(row >= s - s8, d[0:SUBLANES, :], NEG)
            d = head if n == SUBLANES else jnp.concatenate([head, d[SUBLANES:, :]], axis=0)
            e_scr[j, p0:p0 + n, :] = (jnp.exp(d) * qk).astype(BF16)
        return q, k, g

    def finish(j, qkg, red):
        r0 = j * c
        q, k, g = qkg
        v = v_ref[r0:r0 + c, :]
        g_last = g[c - 1:c, :]
        o = _dot_nt((q * jnp.exp(g)).astype(BF16), st[...].astype(BF16))
        parts = []
        for blk in range(c // SUBLANES):
            acc = jnp.zeros((SUBLANES, w), F32)
            for s in range(min(c, (blk + 1) * SUBLANES)):
                p0, s8, n = offs[s]
                lo = p0 + blk * SUBLANES - s8
                acc = acc + red[lo:lo + SUBLANES, :] * v[s:s + 1, :]
            parts.append(acc)
        o = o + jnp.concatenate(parts, axis=0)
        kv = _dot_tn(v.astype(BF16), (k * jnp.exp(g_last - g)).astype(BF16))
        st[...] = st[...] * jnp.exp(g_last) + kv * bd.astype(F32)
        y = _group_norm(o, bd_ref, 64, gain_ref[...]) * gs_ref[r0:r0 + c, :]
        y_ref[r0:r0 + c, :] = y.astype(y_ref.dtype)

    built = [build(j) for j in range(min(2, nb))]
    reds = [_dot(e_scr[0], bd)]
    for j in range(nb):
        if j + 2 < nb:
            built.append(build(j + 2))
        if j + 1 < nb:
            reds.append(_dot(e_scr[j + 1], bd))
        finish(j, built[j], reds[j])

    @pl.when(i == pl.num_programs(1) - 1)
    def _():
        st_ref[...] = st[...]


def _hgrn_rows(c):
    return sum(c - (s // SUBLANES) * SUBLANES for s in range(c))


def _hgrn_prompt(bq, bk, bv, blf, bgs, lw, batch):
    t, w = bq.shape
    c = HGRN_CHUNK
    nb = math.gcd(t // batch // c, HGRN_BLOCKS)
    steps = t // batch // (c * nb)
    blk = pl.BlockSpec((c * nb, w), lambda b, i: (b * steps + i, 0))
    return pl.pallas_call(
        functools.partial(_hgrn_kernel, c=c, nb=nb, w=w),
        grid=(batch, steps),
        in_specs=[blk, blk, blk, blk, blk, _const_spec((1, w)), _const_spec((c, c)), _const_spec((w, w))],
        out_specs=[blk, pl.BlockSpec((None, w, w), lambda b, i: (b, 0, 0))],
        out_shape=[jax.ShapeDtypeStruct((t, w), BF16), jax.ShapeDtypeStruct((batch, w, w), F32)],
        scratch_shapes=[pltpu.VMEM((w, w), F32), pltpu.VMEM((nb, _hgrn_rows(c), w), BF16)],
        compiler_params=_params(("arbitrary", "arbitrary")),
        name="hgrn_prompt",
    )(bq, bk, bv, blf, bgs, lw["hgrn_g"], lw["tri_hgrn"], lw["g64"])


def _cum_kernel(lf_ref, tri_ref, o_ref, carry):
    i = pl.program_id(1)

    @pl.when(i == 0)
    def _():
        carry[...] = jnp.zeros_like(carry)

    lf = lf_ref[...]
    lf2 = jnp.concatenate([lf, lf], axis=0)
    cum = _dot3(lf2, tri_ref[...])[0:ROWS, :] + carry[:, 0:1]
    carry[...] = jnp.broadcast_to(cum[:, cum.shape[1] - 1:], carry.shape)
    pad = jnp.zeros((LANE - ROWS, cum.shape[1]), F32)
    o_ref[...] = jnp.concatenate([-cum, pad], axis=0).T


def _neg_cumsum(lft, tri, batch, tcs):
    rows, t = lft.shape
    npb = t // batch // tcs
    return pl.pallas_call(
        _cum_kernel,
        grid=(batch, npb),
        in_specs=[pl.BlockSpec((rows, tcs), lambda b, i: (0, b * npb + i)), _const_spec((tcs, tcs))],
        out_specs=pl.BlockSpec((tcs, LANE), lambda b, i: (b * npb + i, 0)),
        out_shape=jax.ShapeDtypeStruct((t, LANE), F32),
        scratch_shapes=[pltpu.VMEM((rows, LANE), F32)],
        compiler_params=_params(("arbitrary", "arbitrary")),
        name="fox_cumsum",
    )(lft, tri)


ATTN_QW = 256
ATTN_KC = 128
ATTN_BUFS = 4


def _attn_kernel(qi_ref, ki_ref, qt_ref, k_ref, vt_ref, bq_ref, bk_ref, lam_ref, gain_ref, bd_ref,
                 o_ref, qtm, m_scr, l_scr, acc_scr, bias_scr, s_scr, p_scr, *, r_rows, diff, slopes, t, w):
    j = pl.program_id(1)
    qi = qi_ref[j]
    ki = ki_ref[j]
    dh = w // N_HEADS
    width = w // r_rows
    qw, kc_len = min(ATTN_QW, t), min(ATTN_KC, t)
    key_iota = lax.broadcasted_iota(jnp.int32, (kc_len, qw), 0)

    @pl.when(ki == 0)
    def _():
        qt = qt_ref[...]
        owner = lax.broadcasted_iota(jnp.int32, (w, t), 0) // width
        for r in range(r_rows):
            qtm[r] = jnp.where(owner == r, qt, jnp.zeros_like(qt))
        m_scr[...] = jnp.full_like(m_scr, NEG)
        l_scr[...] = jnp.zeros_like(l_scr)
        acc_scr[...] = jnp.zeros_like(acc_scr)
        if diff:
            for h in range(N_HEADS):
                bias_scr[h] = (slopes[h] * LOG2E) * key_iota.astype(F32)

    def step(masked):
        if diff:
            tile_off = ((ki - qi) * t).astype(F32)
        else:
            for h in range(N_HEADS):
                col = (bk_ref[:, h:h + 1] - bq_ref[0:1, h:h + 1]) * LOG2E
                bias_scr[h] = jnp.broadcast_to(col, (t, LANE))

        def logits(g, r, q0):
            h = r * N_HEADS // r_rows
            n_keys = min(t, q0 + qw) if masked else t
            chunks = range(0, n_keys, kc_len)
            shifts = [slopes[h] * LOG2E * (tile_off + float(k0)) if diff else 0.0 for k0 in chunks]
            m_run = m_scr[r, :, q0:q0 + qw]
            m8 = jnp.full((SUBLANES, qw), NEG, F32)
            for k0, shift in zip(chunks, shifts):
                s = _dot(k_ref[k0:k0 + kc_len, :], qtm[r, :, q0:q0 + qw])
                if diff:
                    s = s + bias_scr[h]
                else:
                    s = s + jnp.concatenate([bias_scr[h, k0:k0 + kc_len, :]] * (qw // LANE), axis=1)
                if masked and k0 + kc_len - 1 > q0:
                    q_iota = lax.broadcasted_iota(jnp.int32, (kc_len, qw), 1)
                    s = jnp.where(key_iota + (k0 - q0) <= q_iota, s, NEG)
                s_scr[g % ATTN_BUFS, k0:k0 + kc_len, :] = s
                m8 = jnp.maximum(m8, jnp.max(s.reshape(kc_len // SUBLANES, SUBLANES, qw), axis=0) + shift)
            m_new = jnp.maximum(m_run, jnp.max(m8, axis=0, keepdims=True))
            return dict(buf=g % ATTN_BUFS, r=r, h=h, q0=q0, n_keys=n_keys, chunks=chunks, shifts=shifts,
                        m_new=m_new, alpha=jnp.exp2(m_run - m_new))

        def weigh(st):
            buf, r, h, q0, m_new, alpha = st["buf"], st["r"], st["h"], st["q0"], st["m_new"], st["alpha"]
            l8 = jnp.zeros((SUBLANES, qw), F32)
            for k0, shift in zip(st["chunks"], st["shifts"]):
                p = jnp.exp2(s_scr[buf, k0:k0 + kc_len, :] - (m_new - shift))
                l8 = l8 + jnp.sum(p.reshape(kc_len // SUBLANES, SUBLANES, qw), axis=0)
                p_scr[buf, k0:k0 + kc_len, :] = p.astype(BF16)
            l_add = jnp.sum(l8, axis=0, keepdims=True)
            pv = _dot(vt_ref[h * dh:(h + 1) * dh, 0:st["n_keys"]], p_scr[buf, 0:st["n_keys"], :])
            m_scr[r, :, q0:q0 + qw] = m_new
            l_scr[r, :, q0:q0 + qw] = alpha * l_scr[r, :, q0:q0 + qw] + l_add
            acc_scr[r, :, q0:q0 + qw] = alpha * acc_scr[r, :, q0:q0 + qw] + pv

        groups = [(r, q0) for r in range(r_rows) for q0 in range(0, t, qw)]
        ahead = ATTN_BUFS - 1
        pending = [logits(g, *groups[g]) for g in range(min(ahead, len(groups)))]
        for g in range(len(groups)):
            if g + ahead < len(groups):
                pending.append(logits(g + ahead, *groups[g + ahead]))
            weigh(pending.pop(0))

    @pl.when(ki < qi)
    def _():
        step(False)

    @pl.when(ki == qi)
    def _():
        step(True)
        heads = []
        for h in range(N_HEADS):
            if diff:
                o_h = (acc_scr[2 * h] * (1.0 / l_scr[2 * h])
                       - lam_ref[...] * (acc_scr[2 * h + 1] * (1.0 / l_scr[2 * h + 1])))
            else:
                o_h = acc_scr[h] * (1.0 / l_scr[h])
            heads.append(o_h)
        out = jnp.concatenate(heads, axis=0).T
        if diff:
            out = _group_norm(out, bd_ref, dh, gain_ref[...])
        o_ref[...] = out.astype(o_ref.dtype)


def _attn_prompt(qt, k, vt, bias, lam, gain, g64, batch, t_blk, r_rows, diff, slopes):
    w = qt.shape[1]
    t = k.shape[0]
    nq = t // batch // t_blk
    pairs = [(a, b) for a in range(nq) for b in range(a + 1)]
    qi = jnp.asarray([p[0] for p in pairs], jnp.int32)
    ki = jnp.asarray([p[1] for p in pairs], jnp.int32)
    oblk = pl.BlockSpec((t_blk, w), lambda b, j, qi, ki: (b * nq + qi[j], 0))
    kblk = pl.BlockSpec((t_blk, w), lambda b, j, qi, ki: (b * nq + ki[j], 0))
    qtblk = pl.BlockSpec((None, w, t_blk), lambda b, j, qi, ki: (b, 0, qi[j]))
    vtblk = pl.BlockSpec((None, w, t_blk), lambda b, j, qi, ki: (b, 0, ki[j]))
    if diff:
        bias = jnp.zeros((ROWS, LANE), F32)
        bias_specs = [_const_spec((ROWS, LANE))] * 2
        bias_scr = pltpu.VMEM((N_HEADS, min(ATTN_KC, t_blk), min(ATTN_QW, t_blk)), F32)
    else:
        bias_specs = [pl.BlockSpec((t_blk, LANE), lambda b, j, qi, ki: (b * nq + qi[j], 0)),
                      pl.BlockSpec((t_blk, LANE), lambda b, j, qi, ki: (b * nq + ki[j], 0))]
        bias_scr = pltpu.VMEM((N_HEADS, t_blk, LANE), F32)
    grid_spec = pltpu.PrefetchScalarGridSpec(
        num_scalar_prefetch=2,
        grid=(batch, len(pairs)),
        in_specs=[qtblk, kblk, vtblk, *bias_specs,
                  _const_spec((1, 1)), _const_spec((1, w)), _const_spec((w, w))],
        out_specs=oblk,
        scratch_shapes=[pltpu.VMEM((r_rows, w, t_blk), BF16),
                        pltpu.VMEM((r_rows, 1, t_blk), F32),
                        pltpu.VMEM((r_rows, 1, t_blk), F32),
                        pltpu.VMEM((r_rows, w // N_HEADS, t_blk), F32),
                        bias_scr,
                        pltpu.VMEM((ATTN_BUFS, t_blk, min(ATTN_QW, t_blk)), F32),
                        pltpu.VMEM((ATTN_BUFS, t_blk, min(ATTN_QW, t_blk)), BF16)])
    return pl.pallas_call(
        functools.partial(_attn_kernel, r_rows=r_rows, diff=diff, slopes=slopes, t=t_blk, w=w),
        grid_spec=grid_spec,
        out_shape=jax.ShapeDtypeStruct((t, w), BF16),
        compiler_params=_params(("arbitrary", "arbitrary")),
        name="diff_prompt" if diff else "fox_prompt",
    )(qi, ki, qt, k, vt, bias, bias, lam, gain, g64)


def _merge_kernel(x_ref, sh_ref, sc_ref, gt_ref, g1_ref, ya_ref, yb_ref, yc_ref, yd_ref,
                  wg_ref, wb_ref, wo_ref, o_ref, *, d, w):
    x = x_ref[...]
    h = _mod_norm(x, g1_ref[...], sc_ref[...], sh_ref[...]).astype(BF16)
    y_refs = (ya_ref, yb_ref, yc_ref, yd_ref)

    def branch(j):
        return (_dot(h, wg_ref[:, j * d:(j + 1) * d]),
                _dot(y_refs[j][...].astype(BF16), wb_ref[j * w:(j + 1) * w, :]))

    merged = jnp.zeros(x.shape, F32)
    nxt = branch(0)
    for j in range(len(y_refs)):
        gate_logits, proj = nxt
        if j + 1 < len(y_refs):
            nxt = branch(j + 1)
        merged = merged + _sigmoid(gate_logits) * proj
    o_ref[...] = x + gt_ref[...] * _dot(merged.astype(BF16), wo_ref[...])


def _merge(x, mod, ys, lw, tm, tiles_per_mod):
    t, d = x.shape
    w = d // 4
    r = mod.shape[1]
    row = lambda i: (i, 0)
    modspec = lambda col: pl.BlockSpec((None, r, d), lambda i: (i // tiles_per_mod, 0, col))
    yspec = pl.BlockSpec((tm, w), row)
    return pl.pallas_call(
        functools.partial(_merge_kernel, d=d, w=w),
        grid=(t // tm,),
        in_specs=[pl.BlockSpec((tm, d), row), modspec(0), modspec(1), modspec(2), _const_spec((1, d)),
                  yspec, yspec, yspec, yspec,
                  _const_spec((d, 4 * d)), _const_spec((d, d)), _const_spec((d, d))],
        out_specs=pl.BlockSpec((tm, d), row),
        out_shape=jax.ShapeDtypeStruct((t, d), F32),
        compiler_params=_params(("arbitrary",)),
        name="merge_out",
    )(x, mod, mod, mod, lw["g1"], *ys, lw["w_gates"], lw["w_branch"], lw["w_out"])


def _ffn_kernel(x_ref, sh_ref, sc_ref, gt_ref, g2_ref, w1_ref, w2_ref, o_ref, *, d, n_chunks):
    x = x_ref[...]
    h = _mod_norm(x, g2_ref[...], sc_ref[...], sh_ref[...]).astype(BF16)
    def up(cidx):
        return _dot(h, w1_ref[:, cidx * d:(cidx + 1) * d])

    acc = jnp.zeros(x.shape, F32)
    nxt = up(0)
    for cidx in range(n_chunks):
        u = jnp.maximum(nxt, 0.0)
        if cidx + 1 < n_chunks:
            nxt = up(cidx + 1)
        acc = acc + _dot((u * u).astype(BF16), w2_ref[cidx * d:(cidx + 1) * d, :])
    o_ref[...] = x + gt_ref[...] * acc


def _ffn(x, mod, lw, tm, tiles_per_mod):
    t, d = x.shape
    dff = lw["w_ff1"].shape[1]
    r = mod.shape[1]
    row = lambda i: (i, 0)
    modspec = lambda col: pl.BlockSpec((None, r, d), lambda i: (i // tiles_per_mod, 0, col))
    return pl.pallas_call(
        functools.partial(_ffn_kernel, d=d, n_chunks=dff // d),
        grid=(t // tm,),
        in_specs=[pl.BlockSpec((tm, d), row), modspec(3), modspec(4), modspec(5), _const_spec((1, d)),
                  _const_spec((d, dff)), _const_spec((dff, d))],
        out_specs=pl.BlockSpec((tm, d), row),
        out_shape=jax.ShapeDtypeStruct((t, d), F32),
        compiler_params=_params(("arbitrary",)),
        name="ffn",
    )(x, mod, mod, mod, lw["g2"], lw["w_ff1"], lw["w_ff2"])


def _conv_step_kernel(state_ref, glu_ref, w_ref, b_ref, g_ref, beta_ref, o_ref):
    n = CONV_W - 1
    acc = glu_ref[...] * w_ref[n:n + 1, :] + b_ref[...]
    for j in range(n):
        acc = acc + state_ref[j] * w_ref[j:j + 1, :]
    o_ref[...] = _ln_silu(acc, g_ref[...], beta_ref[...])


def _conv_step(state_t, glu, lw):
    db, w = glu.shape
    return pl.pallas_call(
        _conv_step_kernel,
        out_shape=jax.ShapeDtypeStruct((db, w), F32),
        compiler_params=_params(None),
        name="conv_step",
    )(state_t, glu, lw["conv_w"], lw["conv_b"], lw["conv_ln_g"], lw["conv_ln_b"])


def _hgrn_step_kernel(s0_ref, qc_ref, kc_ref, lfc_ref, vr_ref, gsr_ref, gain_ref, y_ref, s_ref):
    s_new = jnp.exp(lfc_ref[...]) * s0_ref[...] + kc_ref[...] * vr_ref[...]
    s_ref[...] = s_new
    o = jnp.sum(qc_ref[...] * s_new, axis=2, keepdims=True)
    ms = jnp.mean(o * o, axis=-1, keepdims=True)
    y_ref[...] = o * lax.rsqrt(ms + EPS) * gain_ref[...] * gsr_ref[...]


def _hgrn_step(s0, bq, bk, bv, blf, bgs, gain):
    db, nh, dk, dv = s0.shape
    col = lambda a: a.reshape(db, nh, dk, 1)
    rowv = lambda a: a.reshape(db, nh, 1, dv)
    y, s = pl.pallas_call(
        _hgrn_step_kernel,
        out_shape=[jax.ShapeDtypeStruct((db, nh, 1, dv), F32), jax.ShapeDtypeStruct(s0.shape, F32)],
        compiler_params=_params(None),
        name="hgrn_step",
    )(s0, col(bq), col(bk), col(blf), rowv(bv), rowv(bgs), gain.reshape(1, 1, 1, dv))
    return y.reshape(db, nh * dv), s


def _suffix_kernel(lf_ref, m_ref, o_ref):
    o_ref[...] = _dot3(lf_ref[...], m_ref[...])


def _suffix_matrix(page):
    src = np.arange(page)[:, None]
    dst = np.arange(2 * page)[None, :]
    return jnp.asarray((dst >= page) | (src > dst), BF16)


def _suffix_table(logf_rows, page):
    n = logf_rows.shape[0]
    tr = math.gcd(n, 1024)
    out = pl.pallas_call(
        _suffix_kernel,
        grid=(n // tr,),
        in_specs=[pl.BlockSpec((tr, page), lambda i: (i, 0)), _const_spec((page, 2 * page))],
        out_specs=pl.BlockSpec((tr, 2 * page), lambda i: (i, 0)),
        out_shape=jax.ShapeDtypeStruct((n, 2 * page), F32),
        compiler_params=_params(("arbitrary",)),
        name="fox_suffix_table",
    )(logf_rows, _suffix_matrix(page))
    return out.reshape(n // ROWS, ROWS, 2 * page)


def _decode_kernel(pt_ref, qm_ref, kn_ref, vn_ref, cnew_ref, slope_ref, lam_ref, gain_ref,
                   kc_ref, vc_ref, wt_ref, o_ref,
                   kbuf, vbuf, wbuf, sem, m_scr, l_scr, acc_scr, carry, p_scr, alpha_scr,
                   *, fox, scale, g_pages, n_groups, n_seq, page, nbuf, page_base, w):
    n_it = n_seq * n_groups
    past = n_groups * g_pages * page
    n_live = N_HEADS if fox else 2 * N_HEADS
    dh = w // N_HEADS
    lane0 = lax.broadcasted_iota(jnp.int32, (dh, page), 1) == 0

    def copies(it, slot):
        b = it // n_groups
        c = n_groups - 1 - it % n_groups
        out = []
        for g in range(g_pages):
            pg = page_base + pt_ref[b, c * g_pages + g]
            out.append(pltpu.make_async_copy(kc_ref.at[pg], kbuf.at[slot, g], sem.at[0, slot]))
            out.append(pltpu.make_async_copy(vc_ref.at[pg], vbuf.at[slot, g], sem.at[1, slot]))
            if fox:
                out.append(pltpu.make_async_copy(wt_ref.at[pg], wbuf.at[slot, g], sem.at[2, slot]))
        return out

    ahead = nbuf - 2

    def score(it):
        slot = it % nbuf
        b = it // n_groups
        c = n_groups - 1 - it % n_groups
        first = c == n_groups - 1
        qm = qm_ref[b]
        s = jnp.concatenate([_dot(qm, kbuf[slot, g]) for g in range(g_pages)], axis=1) * scale
        if fox:
            parts = [None] * g_pages
            run = jnp.where(first, 0.0, carry[...])
            for g in reversed(range(g_pages)):
                parts[g] = wbuf[slot, g, :, 0:page] + run
                run = run + wbuf[slot, g, :, page:2 * page]
            carry[...] = run
            s = s + jnp.concatenate(parts, axis=1) + cnew_ref[b]
        else:
            pos = (c * (g_pages * page) + lax.broadcasted_iota(jnp.int32, (1, g_pages * page), 1)).astype(F32)
            s = s + slope_ref[...] * (pos - float(past))
        m_prev = jnp.where(first, jnp.sum(qm * kn_ref[b], axis=-1, keepdims=True) * scale, m_scr[...])
        l_prev = jnp.where(first, 1.0, l_scr[...])
        m_new = jnp.maximum(m_prev, jnp.max(s, axis=-1, keepdims=True))
        alpha = jnp.exp(m_prev - m_new)
        p = jnp.exp(s - m_new)
        l_scr[...] = alpha * l_prev + jnp.sum(p, axis=-1, keepdims=True)
        m_scr[...] = m_new
        p_scr[...] = p
        alpha_scr[...] = alpha

    def weigh(jt):
        slot = jt % nbuf
        b = jt // n_groups
        c = n_groups - 1 - jt % n_groups
        first = c == n_groups - 1
        l_now = l_scr[...]
        for r in range(n_live):
            h = r * N_HEADS // n_live
            rows = slice(h * dh, (h + 1) * dh)
            start = jnp.where(lane0, vn_ref[b, rows, :], 0.0)
            acc = jnp.where(first, start, acc_scr[r]) * alpha_scr[r:r + 1, :]
            for g in range(g_pages):
                acc = acc + vbuf[slot, g, rows, :] * p_scr[r:r + 1, g * page:(g + 1) * page]
            acc_scr[r] = acc
        return l_now, b, c == 0

    def finish(l_now, b):
        cols = [jnp.sum(acc_scr[r], axis=1, keepdims=True) / l_now[r:r + 1, :] for r in range(n_live)]
        heads = []
        for h in range(N_HEADS):
            if fox:
                o_h = cols[h]
            else:
                o_h = cols[2 * h] - lam_ref[...] * cols[2 * h + 1]
                ms = jnp.mean(o_h * o_h, axis=0, keepdims=True)
                o_h = o_h * lax.rsqrt(ms + EPS) * gain_ref[h * dh:(h + 1) * dh, :]
            heads.append(o_h)
        o_ref[b] = jnp.concatenate(heads, axis=0)

    for it0 in range(min(ahead, n_it)):
        for cp in copies(it0, it0 % nbuf):
            cp.start()

    for ref in (p_scr, alpha_scr, acc_scr, carry, m_scr):
        ref[...] = jnp.zeros_like(ref)
    l_scr[...] = jnp.ones_like(l_scr)

    def body(it, _):
        @pl.when(it + ahead < n_it)
        def _():
            for cp in copies(it + ahead, (it + ahead) % nbuf):
                cp.start()

        @pl.when(it < n_it)
        def _():
            for cp in copies(it, it % nbuf):
                cp.wait()

        l_now, b, last = weigh(jnp.maximum(it - 1, 0))
        score(jnp.minimum(it, n_it - 1))

        @pl.when(jnp.logical_and(last, it > 0))
        def _():
            finish(l_now, b)

        return 0

    lax.fori_loop(0, n_it + 1, body, 0)


def _decode_attn(page_table, qm, k_new, v_new, cnew, slopes, lam, gain_col, kt_cache, vt_cache, w_table,
                 layer, n_pool, fox, scale):
    db, n_pages = page_table.shape
    w, page = kt_cache.shape[1], kt_cache.shape[2]
    g_pages = math.gcd(n_pages, 8)
    nbuf = 5
    n_groups = n_pages // g_pages
    full3 = lambda shape: pl.BlockSpec(shape, lambda i, pt: (0, 0, 0))
    full2 = lambda shape: pl.BlockSpec(shape, lambda i, pt: (0, 0))
    any_spec = pl.BlockSpec(memory_space=pl.ANY)
    grid_spec = pltpu.PrefetchScalarGridSpec(
        num_scalar_prefetch=1,
        grid=(1,),
        in_specs=[full3((db, ROWS, w)), full3((db, 1, w)), full3((db, w, 1)), full3((db, ROWS, 1)),
                  full2((ROWS, 1)), full2((1, 1)), full2((w, 1)),
                  any_spec, any_spec, any_spec],
        out_specs=full3((db, w, 1)),
        scratch_shapes=[pltpu.VMEM((nbuf, g_pages, w, page), F32),
                        pltpu.VMEM((nbuf, g_pages, w, page), F32),
                        pltpu.VMEM((nbuf, g_pages, ROWS, 2 * page), F32),
                        pltpu.SemaphoreType.DMA((3, nbuf)),
                        pltpu.VMEM((ROWS, 1), F32), pltpu.VMEM((ROWS, 1), F32),
                        pltpu.VMEM((ROWS, w // N_HEADS, page), F32), pltpu.VMEM((ROWS, page), F32),
                        pltpu.VMEM((ROWS, g_pages * page), F32), pltpu.VMEM((ROWS, 1), F32)])
    out = pl.pallas_call(
        functools.partial(_decode_kernel, fox=fox, scale=scale, g_pages=g_pages,
                          n_groups=n_groups, n_seq=db, page=page, nbuf=nbuf, page_base=layer * n_pool, w=w),
        grid_spec=grid_spec,
        out_shape=jax.ShapeDtypeStruct((db, w, 1), F32),
        compiler_params=_params(("arbitrary",)),
        name="fox_decode" if fox else "diff_decode",
    )(page_table, qm, k_new.reshape(db, 1, w), v_new.reshape(db, w, 1), cnew, slopes, lam, gain_col,
      kt_cache, vt_cache, w_table)
    return out.reshape(db, w)


def _block_diag_ones(w, n):
    idx = np.arange(w) // n
    return jnp.asarray(idx[:, None] == idx[None, :], BF16)


def _row_mask(w, r_rows):
    lane = np.arange(w) // (w // r_rows)
    return jnp.asarray(np.arange(ROWS)[:, None] == lane[None, :], F32)


def _alibi_slopes(n):
    return jnp.asarray(2.0 ** (-8.0 * np.arange(1, n + 1) / n), F32)


def _layer_weights(l, p, lbs, d):
    w = d // 4
    o_f = 9 * w
    o_d = o_f + N_HEADS
    w_in_t = p["w_in"].transpose(2, 0, 1)
    wt_abc = w_in_t[:o_f, l]
    wt_f = w_in_t[o_f:o_d, l]
    wt_rest = w_in_t[o_d:, l]
    lb = lbs[l]
    lbf = jnp.maximum(lb, LB_FLOOR)
    tile = lambda a, n: jnp.tile(a, n).reshape(1, w)
    conv_w = jnp.concatenate([p["conv_w"][l], jnp.zeros((CONV_HALO - CONV_W, w), F32)], axis=0)
    lam_init = 0.8 - 0.6 * math.exp(-0.3 * l)
    lam = (jnp.exp(jnp.sum(p["lam_q1"][l] * p["lam_k1"][l])) - jnp.exp(jnp.sum(p["lam_q2"][l] * p["lam_k2"][l]))
           + lam_init).astype(F32)
    return dict(
        g1=p["norm1_g"][l].reshape(1, d), g2=p["norm2_g"][l].reshape(1, d),
        w_abc=_weight_prep(wt_abc, 0, o_f), w_d=_weight_prep(wt_rest, 0, 3 * w),
        w_gates=_weight_prep(wt_rest, 3 * w, 4 * d),
        w_ft=jnp.pad(wt_f, ((0, ROWS - N_HEADS), (0, 0))).astype(BF16),
        bf_col=jnp.pad(p["b_fox_f"][l], (0, ROWS - N_HEADS)).reshape(ROWS, 1),
        one_minus_lb=(1.0 - lb).reshape(1, w), dlb=(lbf - lb).reshape(1, w),
        fqn=tile(p["fox_qn_g"][l], N_HEADS), fkn=tile(p["fox_kn_g"][l], N_HEADS),
        dqn=tile(p["diff_qn_g"][l], 2 * N_HEADS), dkn=tile(p["diff_kn_g"][l], 2 * N_HEADS),
        hgrn_g=tile(p["hgrn_norm_g"][l], N_HEADS), hgrn_g1=p["hgrn_norm_g"][l],
        diff_g=tile(p["diff_norm_g"][l], N_HEADS) * (1.0 - lam_init),
        conv_w=conv_w, conv_b=p["conv_b"][l].reshape(1, w),
        conv_ln_g=p["conv_ln_g"][l].reshape(1, w), conv_ln_b=p["conv_ln_b"][l].reshape(1, w),
        lam=lam.reshape(1, 1),
        w_branch=p["w_branch"][l].reshape(d, d).astype(BF16),
        w_out=p["w_out"][l].astype(BF16),
        w_ff1=p["w_ff1"][l].astype(BF16), w_ff2=p["w_ff2"][l].astype(BF16),
        ones_w=jnp.ones((1, w), F32),
    )


def _hgrn_state_from_transposed(st, w):
    b = st.shape[0]
    dh = w // N_HEADS
    st5 = st.reshape(b, N_HEADS, dh, N_HEADS, dh)
    return jnp.stack([st5[:, h, :, h, :] for h in range(N_HEADS)], axis=1).transpose(0, 1, 3, 2)


def kernel(x_prompt, x_sample, cache_fox_k, cache_fox_v, cache_fox_logf, cache_diff_k, cache_diff_v,
           state_conv, state_hgrn, page_table, c_prompt, c_sample, w_ada, b_ada, norm1_g, norm2_g, w_in,
           b_fox_f, lb_logits, hgrn_norm_g, conv_w, conv_b, conv_ln_g, conv_ln_b, fox_qn_g, fox_kn_g,
           diff_qn_g, diff_kn_g, lam_q1, lam_k1, lam_q2, lam_k2, diff_norm_g, w_branch, w_out, w_ff1, w_ff2):
    p = dict(norm1_g=norm1_g, norm2_g=norm2_g, w_in=w_in, b_fox_f=b_fox_f, hgrn_norm_g=hgrn_norm_g,
             conv_w=conv_w, conv_b=conv_b, conv_ln_g=conv_ln_g, conv_ln_b=conv_ln_b, fox_qn_g=fox_qn_g,
             fox_kn_g=fox_kn_g, diff_qn_g=diff_qn_g, diff_kn_g=diff_kn_g, lam_q1=lam_q1, lam_k1=lam_k1,
             lam_q2=lam_q2, lam_k2=lam_k2, diff_norm_g=diff_norm_g, w_branch=w_branch, w_out=w_out,
             w_ff1=w_ff1, w_ff2=w_ff2)
    batch, seq, d = x_prompt.shape
    db = x_sample.shape[0]
    assert x_sample.shape[1] == 1, "the sample group advances one token per step"
    depth = w_in.shape[0]
    w = d // 4
    dh = w // N_HEADS
    n_pool, page = cache_fox_k.shape[1], cache_fox_k.shape[2]
    n_pages = page_table.shape[1]

    lbs = jax.nn.softmax(lb_logits.astype(F32), axis=0)
    lbs = jnp.cumsum(lbs, axis=0) - lbs[0]
    lws = [_layer_weights(l, p, lbs, d) for l in range(depth)]
    g64 = _block_diag_ones(w, dh)
    g32 = _block_diag_ones(w, dh // 2)
    for lw in lws:
        lw["g64"], lw["g32"] = g64, g32

    rows = -(-(batch + db) // 8) * 8
    c_all = jnp.concatenate([c_prompt, c_sample, jnp.zeros((rows - batch - db, d), F32)], axis=0)
    mod = _ada(c_all, w_ada, b_ada)
    mod_p = mod[:, :batch].reshape(depth, batch, 1, 6 * d)
    mod_s = mod[:, batch:batch + db].reshape(depth, 1, db, 6 * d)

    t = batch * seq
    tm = math.gcd(seq, 512)
    t_attn = math.gcd(seq, 512)
    tcs = math.gcd(seq, 512)
    tri_cum = jnp.asarray(np.triu(np.ones((tcs, tcs))), BF16)
    tri_hgrn = jnp.asarray(np.tril(np.ones((HGRN_CHUNK, HGRN_CHUNK))), BF16)
    slope_vals = tuple(float(2.0 ** (-8.0 * (h + 1) / N_HEADS)) for h in range(N_HEADS))
    slopes = jnp.asarray(slope_vals, F32)

    def cache_layout(a):
        return a.reshape(a.shape[0], a.shape[1], N_HEADS, dh, a.shape[3]).transpose(0, 1, 4, 2, 3)

    x = x_prompt.reshape(t, d)
    kv_p = [jnp.zeros((depth, batch, w, seq), F32) for _ in range(4)]
    p_conv, p_hgrn, p_logf = [], [], []
    for l, lw in enumerate(lws):
        lw["tri_hgrn"] = tri_hgrn
        outs = _inproj(x, mod_p[l], lw, tm, batch, l, depth, kv_p)
        glu, bq, bk, bv, blf, bgs, ckb, dkb = outs[:8]
        kv_p = outs[8:12]
        cqtb, cvtb, dqtb, dvtb, clft = outs[12:]
        ya = _conv_prompt(glu, lw, batch, math.gcd(seq, 256))
        yb, st = _hgrn_prompt(bq, bk, bv, blf, bgs, lw, batch)
        fox_bias = _neg_cumsum(clft, tri_cum, batch, tcs)
        yc = _attn_prompt(cqtb, ckb, cvtb, fox_bias, lw["lam"], lw["ones_w"], g64, batch, t_attn,
                          N_HEADS, False, slope_vals)
        yd = _attn_prompt(dqtb, dkb, dvtb, None, lw["lam"], lw["diff_g"], g64, batch, t_attn,
                          2 * N_HEADS, True, slope_vals)
        x = _merge(x, mod_p[l], (ya, yb, yc, yd), lw, tm, seq // tm)
        x = _ffn(x, mod_p[l], lw, tm, seq // tm)
        p_conv.append(glu.reshape(batch, seq, w)[:, seq - (CONV_W - 1):])
        p_hgrn.append(_hgrn_state_from_transposed(st, w))
        p_logf.append(clft[:N_HEADS].reshape(N_HEADS, batch, seq).transpose(1, 2, 0))
    y_prompt = x.reshape(batch, seq, d)
    p_out = [jnp.stack(p_conv), jnp.stack(p_hgrn), cache_layout(kv_p[0]), cache_layout(kv_p[1]),
             jnp.stack(p_logf), cache_layout(kv_p[2]), cache_layout(kv_p[3])]

    pool_t = lambda a: a.transpose(0, 1, 3, 4, 2).reshape(depth * n_pool, w, page)
    fox_kt, fox_vt = pool_t(cache_fox_k), pool_t(cache_fox_v)
    diff_kt, diff_vt = pool_t(cache_diff_k), pool_t(cache_diff_v)
    logf_rows = jnp.pad(cache_fox_logf.transpose(0, 1, 3, 2), ((0, 0), (0, 0), (0, ROWS - N_HEADS), (0, 0)))
    w_table = _suffix_table(logf_rows.reshape(depth * n_pool * ROWS, page), page)
    mask4 = _row_mask(w, N_HEADS)
    mask8 = _row_mask(w, 2 * N_HEADS)
    slope_col = jnp.repeat(slopes, 2).reshape(ROWS, 1)
    zero_col = jnp.zeros((db, ROWS, 1), F32)
    head_rows = jnp.arange(ROWS)[:, None] < N_HEADS
    conv_t = state_conv.transpose(0, 2, 1, 3)

    x = x_sample.reshape(db, d)
    kv_s = [jnp.zeros((depth, 1, w, db), F32) for _ in range(4)]
    s_conv, s_hgrn, s_logf = [], [], []
    for l, lw in enumerate(lws):
        outs = _inproj(x, mod_s[l], lw, db, 1, l, depth, kv_s)
        glu, bq, bk, bv, blf, bgs = outs[:6]
        kv_s = outs[8:12]
        cqb, dqb = outs[12][0].T, outs[14][0].T
        clft = outs[16]
        ck, cv, dk, dv = [a[l, 0].T for a in kv_s]
        ya = _conv_step(conv_t[l], glu, lw)
        yb, s_new = _hgrn_step(state_hgrn[l], bq, bk, bv, blf, bgs, lw["hgrn_g1"])
        cnew = jnp.where(head_rows, clft, 0.0).T.reshape(db, ROWS, 1)
        qm_fox = (cqb.astype(F32) * (1.0 / LOG2E))[:, None, :] * mask4[None]
        yc = _decode_attn(page_table, qm_fox, ck, cv, cnew, slope_col, lw["lam"], lw["ones_w"].reshape(w, 1),
                          fox_kt, fox_vt, w_table, l, n_pool, True, 1.0)
        qm_diff = (dqb.astype(F32) * (1.0 / LOG2E))[:, None, :] * mask8[None]
        yd = _decode_attn(page_table, qm_diff, dk, dv, zero_col, slope_col, lw["lam"], lw["diff_g"].reshape(w, 1),
                          diff_kt, diff_vt, w_table, l, n_pool, False, 1.0)
        x = _merge(x, mod_s[l], (ya, yb, yc, yd), lw, db, 1)
        x = _ffn(x, mod_s[l], lw, db, 1)
        s_conv.append(jnp.concatenate([conv_t[l][1:], glu[None]], axis=0).transpose(1, 0, 2))
        s_hgrn.append(s_new)
        s_logf.append(clft[:N_HEADS].T.reshape(db, 1, N_HEADS))
    y_sample = x.reshape(db, 1, d)
    sample_layout = lambda a: cache_layout(a).transpose(0, 2, 1, 3, 4)
    s_out = [jnp.stack(s_conv), jnp.stack(s_hgrn), sample_layout(kv_s[0]), sample_layout(kv_s[1]),
             jnp.stack(s_logf), sample_layout(kv_s[2]), sample_layout(kv_s[3])]
    return (y_prompt, y_sample, *p_out, *s_out)
```

```python
import functools
import math

import numpy as np
import jax
import jax.numpy as jnp
from jax import lax
from jax.experimental import pallas as pl
from jax.experimental.pallas import tpu as pltpu

F32 = jnp.float32
BF16 = jnp.bfloat16

EPS = 1e-6
NEG = -1e30
LOG2E = math.log2(math.e)
LB_FLOOR = 1e-30
CONV_W = 31
N_HEADS = 4
HGRN_CHUNK = 64
HGRN_BLOCKS = 4
CONV_HALO = 32
ROWS = 8
LANE = 128
SUBLANES = 8
V7X_VMEM_LIMIT = 56 * 2**20


def _params(sem, vmem=V7X_VMEM_LIMIT):
    return pltpu.CompilerParams(dimension_semantics=sem, vmem_limit_bytes=vmem)


def _const_spec(shape):
    n = len(shape)
    return pl.BlockSpec(shape, lambda *_: (0,) * n)


def _dot(a, b):
    return jnp.dot(a, b, preferred_element_type=F32)


def _dot_nt(a, b):
    return lax.dot_general(a, b, (((1,), (1,)), ((), ())), preferred_element_type=F32)


def _dot_tn(a, b):
    return lax.dot_general(a, b, (((0,), (0,)), ((), ())), preferred_element_type=F32)


def _split3(a):
    hi = a.astype(BF16)
    r = a - hi.astype(F32)
    mid = r.astype(BF16)
    lo = (r - mid.astype(F32)).astype(BF16)
    return hi, mid, lo


def _dot3(a, m):
    hi, mid, lo = _split3(a)
    return _dot(hi, m) + _dot(mid, m) + _dot(lo, m)


def _dot3_left(m, a):
    hi, mid, lo = _split3(a)
    return _dot(m, hi) + _dot(m, mid) + _dot(m, lo)


def _sigmoid(x):
    return jax.nn.sigmoid(x)


def _silu(x):
    return x * jax.nn.sigmoid(x)


def _log_sigmoid(x):
    return jnp.minimum(x, 0.0) - jnp.log1p(jnp.exp(-jnp.abs(x)))


def _mod_norm(x, g, sc, sh):
    ms = jnp.mean(x * x, axis=-1, keepdims=True)
    return (x * lax.rsqrt(ms + EPS) * g) * (1.0 + sc) + sh


def _group_norm(z, ones_ref, n, gain):
    ss = _dot((z * z).astype(BF16), ones_ref[...]) * (1.0 / n)
    return z * lax.rsqrt(ss + EPS) * gain


def _ada_kernel(c_ref, w_ref, b_ref, o_ref):
    a = _silu(c_ref[...]).astype(BF16)
    o_ref[...] = _dot(a, w_ref[...].astype(BF16)) + b_ref[...]


def _ada(c_all, w_ada, b_ada):
    depth, d, n = w_ada.shape
    rows = c_all.shape[0]
    tn = d
    return pl.pallas_call(
        _ada_kernel,
        grid=(depth, n // tn),
        in_specs=[pl.BlockSpec((rows, d), lambda l, j: (0, 0)),
                  pl.BlockSpec((None, d, tn), lambda l, j: (l, 0, j)),
                  pl.BlockSpec((None, 1, tn), lambda l, j: (l, 0, j))],
        out_specs=pl.BlockSpec((None, rows, tn), lambda l, j: (l, 0, j)),
        out_shape=jax.ShapeDtypeStruct((depth, rows, n), F32),
        compiler_params=_params(("arbitrary", "arbitrary")),
        name="ada_mod",
    )(c_all, w_ada, b_ada.reshape(depth, 1, n))


W_PREP_ROWS = 256


def _wprep_kernel(w_ref, o_ref):
    o_ref[...] = w_ref[...].T.astype(BF16)


def _weight_prep(wt, row0, n_rows):
    d = wt.shape[1]
    bn = W_PREP_ROWS
    assert row0 % bn == 0 and n_rows % bn == 0
    return pl.pallas_call(
        _wprep_kernel,
        grid=(n_rows // bn,),
        in_specs=[pl.BlockSpec((bn, d), lambda i: (row0 // bn + i, 0))],
        out_specs=pl.BlockSpec((d, bn), lambda i: (0, i)),
        out_shape=jax.ShapeDtypeStruct((d, n_rows), BF16),
        compiler_params=_params(("arbitrary",)),
        name="weight_prep",
    )(wt)


N_INPROJ_IN = 16
N_INPROJ_STATE0 = 8


def _inproj_kernel(*refs, w, n_alias):
    (x_ref, sh_ref, sc_ref, g1_ref, wabc_ref, wd_ref, wft_ref, bfc_ref,
     oml_ref, dlb_ref, fqn_ref, fkn_ref, dqn_ref, dkn_ref, g64_ref, g32_ref) = refs[:N_INPROJ_IN]
    (glu_o, bq_o, bk_o, bv_o, blf_o, bgs_o, ckb_o, dkb_o,
     ckt_o, cvt_o, dkt_o, dvt_o, cqtb_o, cvtb_o, dqtb_o, dvtb_o, clft_o) = refs[N_INPROJ_IN + n_alias:]
    h = _mod_norm(x_ref[...], g1_ref[...], sc_ref[...], sh_ref[...]).astype(BF16)

    def seg(j):
        if j < 9:
            return _dot(h, wabc_ref[:, j * w:(j + 1) * w])
        return _dot(h, wd_ref[:, (j - 9) * w:(j - 8) * w])

    z = [seg(j) for j in range(12)]
    glu_o[...] = z[0] * _sigmoid(z[1])
    bq_o[...] = _silu(z[2])
    key = oml_ref[...] * (1.0 / (1.0 + jnp.exp(z[3]))) - dlb_ref[...]
    bk_o[...] = key
    blf_o[...] = jnp.log1p(-key)
    bv_o[...] = z[4]
    bgs_o[...] = _silu(z[5])
    cq = _group_norm(z[6], g64_ref, 64, fqn_ref[...])
    cqtb_o[...] = (cq * (64 ** -0.5 * LOG2E)).T.astype(BF16)
    ck = _group_norm(z[7], g64_ref, 64, fkn_ref[...])
    ckb_o[...] = ck.astype(BF16)
    ckt_o[...] = ck.T
    cvt = z[8].T
    cvt_o[...] = cvt
    cvtb_o[...] = cvt.astype(BF16)
    clft_o[...] = _log_sigmoid(_dot_nt(wft_ref[...], h) + bfc_ref[...])
    dq = _group_norm(z[9], g32_ref, 32, dqn_ref[...])
    dqtb_o[...] = (dq * (32 ** -0.5 * LOG2E)).T.astype(BF16)
    dk = _group_norm(z[10], g32_ref, 32, dkn_ref[...])
    dkb_o[...] = dk.astype(BF16)
    dkt_o[...] = dk.T
    dvt = z[11].T
    dvt_o[...] = dvt
    dvtb_o[...] = dvt.astype(BF16)


def _inproj(x, mod, lw, tm, batch, layer, depth, prev_states):
    t, d = x.shape
    w = d // 4
    s = t // batch
    tpb = s // tm
    r = mod.shape[1]
    row = lambda i: (i, 0)
    f32w = jax.ShapeDtypeStruct((t, w), F32)
    bf16w = jax.ShapeDtypeStruct((t, w), BF16)
    state = jax.ShapeDtypeStruct((depth, batch, w, s), F32)
    wspec = pl.BlockSpec((tm, w), row)
    vec = _const_spec((1, w))
    state_spec = pl.BlockSpec((None, None, w, tm), lambda i: (layer, i // tpb, 0, i % tpb))
    tspec = pl.BlockSpec((None, w, tm), lambda i: (i // tpb, 0, i % tpb))
    out_shapes = ([f32w] * 6 + [bf16w] * 2 + [state] * 4 + [jax.ShapeDtypeStruct((batch, w, s), BF16)] * 4
                  + [jax.ShapeDtypeStruct((ROWS, t), F32)])
    out_specs = [wspec] * 8 + [state_spec] * 4 + [tspec] * 4 + [pl.BlockSpec((ROWS, tm), lambda i: (0, i))]
    in_specs = [pl.BlockSpec((tm, d), row),
                pl.BlockSpec((None, r, d), lambda i: (i // tpb, 0, 0)),
                pl.BlockSpec((None, r, d), lambda i: (i // tpb, 0, 1)),
                _const_spec((1, d)),
                _const_spec((d, 9 * w)), _const_spec((d, 3 * w)),
                _const_spec((ROWS, d)), _const_spec((ROWS, 1)),
                vec, vec, vec, vec, vec, vec, _const_spec((w, w)), _const_spec((w, w))]
    args = [x, mod, mod, lw["g1"], lw["w_abc"], lw["w_d"], lw["w_ft"], lw["bf_col"],
            lw["one_minus_lb"], lw["dlb"], lw["fqn"], lw["fkn"], lw["dqn"], lw["dkn"], lw["g64"], lw["g32"]]
    assert len(args) == N_INPROJ_IN
    aliases = {}
    for n, a in enumerate(prev_states):
        aliases[len(args)] = N_INPROJ_STATE0 + n
        args.append(a)
        in_specs.append(pl.BlockSpec(memory_space=pl.ANY))
    return pl.pallas_call(
        functools.partial(_inproj_kernel, w=w, n_alias=len(aliases)),
        grid=(t // tm,),
        in_specs=in_specs,
        out_specs=out_specs,
        out_shape=out_shapes,
        input_output_aliases=aliases,
        compiler_params=_params(("arbitrary",)),
        name="in_proj",
    )(*args)


def _ln_silu(acc, g, b):
    mu = jnp.mean(acc, axis=-1, keepdims=True)
    cen = acc - mu
    var = jnp.mean(cen * cen, axis=-1, keepdims=True)
    return _silu(cen * lax.rsqrt(var + EPS) * g + b)


def _conv_kernel(cur_ref, prev_ref, w_ref, b_ref, g_ref, beta_ref, o_ref, win, *, tc, sub):
    i = pl.program_id(1)
    rows = tc + CONV_HALO
    halo = prev_ref[tc - CONV_HALO:tc, :]
    win[0, 0:CONV_HALO, :] = jnp.where(i > 0, halo, 0.0)
    win[0, CONV_HALO:rows, :] = cur_ref[...]
    for r in range(1, SUBLANES):
        win[r, 0:rows - SUBLANES, :] = win[0, r:r + rows - SUBLANES, :]
    off = CONV_HALO - (CONV_W - 1)
    for r0 in range(0, tc, sub):
        acc = jnp.broadcast_to(b_ref[...], (sub, b_ref.shape[1]))
        for j in range(CONV_W):
            shift = (off + j) % SUBLANES
            lo = r0 + off + j - shift
            acc = acc + win[shift, lo:lo + sub, :] * w_ref[j:j + 1, :]
        o_ref[r0:r0 + sub, :] = _ln_silu(acc, g_ref[...], beta_ref[...]).astype(o_ref.dtype)


def _conv_prompt(glu, lw, batch, tc):
    t, w = glu.shape
    npb = t // batch // tc
    vec = _const_spec((1, w))
    return pl.pallas_call(
        functools.partial(_conv_kernel, tc=tc, sub=min(64, tc)),
        grid=(batch, npb),
        in_specs=[pl.BlockSpec((tc, w), lambda b, i: (b * npb + i, 0)),
                  pl.BlockSpec((tc, w), lambda b, i: (b * npb + jnp.maximum(i - 1, 0), 0)),
                  _const_spec((CONV_HALO, w)), vec, vec, vec],
        out_specs=pl.BlockSpec((tc, w), lambda b, i: (b * npb + i, 0)),
        out_shape=jax.ShapeDtypeStruct((t, w), BF16),
        scratch_shapes=[pltpu.VMEM((SUBLANES, tc + CONV_HALO, w), F32)],
        compiler_params=_params(("arbitrary", "arbitrary")),
        name="conv_prompt",
    )(glu, glu, lw["conv_w"], lw["conv_b"], lw["conv_ln_g"], lw["conv_ln_b"])


def _hgrn_kernel(q_ref, k_ref, v_ref, lf_ref, gs_ref, gain_ref, tri_ref, bd_ref,
                 y_ref, st_ref, st, e_scr, *, c, nb, w):
    i = pl.program_id(1)

    @pl.when(i == 0)
    def _():
        st[...] = jnp.zeros_like(st)

    bd = bd_ref[...]
    row = lax.broadcasted_iota(jnp.int32, (SUBLANES, w), 0)
    offs = []
    pos = 0
    for s in range(c):
        s8 = (s // SUBLANES) * SUBLANES
        offs.append((pos, s8, c - s8))
        pos += c - s8

    def build(j):
        r0 = j * c
        q = q_ref[r0:r0 + c, :]
        k = k_ref[r0:r0 + c, :]
        g = _dot3_left(tri_ref[...], lf_ref[r0:r0 + c, :])
        for s in range(c):
            p0, s8, n = offs[s]
            d = g[s8:, :] - g[s:s + 1, :]
            qk = q[s8:, :] * k[s:s + 1, :]
            head = jnp.where(row >= s - s8, d[0:SUBLANES, :], NEG)
            d = head if n == SUBLANES else jnp.concatenate([head, d[SUBLANES:, :]], axis=0)
            e_scr[j, p0:p0 + n, :] = (jnp.exp(d) * qk).astype(BF16)
        return q, k, g

    def finish(j, qkg, red):
        r0 = j * c
        q, k, g = qkg
        v = v_ref[r0:r0 + c, :]
        g_last = g[c - 1:c, :]
        o = _dot_nt((q * jnp.exp(g)).astype(BF16), st[...].astype(BF16))
        parts = []
        for blk in range(c // SUBLANES):
            acc = jnp.zeros((SUBLANES, w), F32)
            for s in range(min(c, (blk + 1) * SUBLANES)):
                p0, s8, n = offs[s]
                lo = p0 + blk * SUBLANES - s8
                acc = acc + red[lo:lo + SUBLANES, :] * v[s:s + 1, :]
            parts.append(acc)
        o = o + jnp.concatenate(parts, axis=0)
        kv = _dot_tn(v.astype(BF16), (k * jnp.exp(g_last - g)).astype(BF16))
        st[...] = st[...] * jnp.exp(g_last) + kv * bd.astype(F32)
        y = _group_norm(o, bd_ref, 64, gain_ref[...]) * gs_ref[r0:r0 + c, :]
        y_ref[r0:r0 + c, :] = y.astype(y_ref.dtype)

    built = [build(j) for j in range(min(2, nb))]
    reds = [_dot(e_scr[0], bd)]
    for j in range(nb):
        if j + 2 < nb:
            built.append(build(j + 2))
        if j + 1 < nb:
            reds.append(_dot(e_scr[j + 1], bd))
        finish(j, built[j], reds[j])

    @pl.when(i == pl.num_programs(1) - 1)
    def _():
        st_ref[...] = st[...]


def _hgrn_rows(c):
    return sum(c - (s // SUBLANES) * SUBLANES for s in range(c))


def _hgrn_prompt(bq, bk, bv, blf, bgs, lw, batch):
    t, w = bq.shape
    c = HGRN_CHUNK
    nb = math.gcd(t // batch // c, HGRN_BLOCKS)
    steps = t // batch // (c * nb)
    blk = pl.BlockSpec((c * nb, w), lambda b, i: (b * steps + i, 0))
    return pl.pallas_call(
        functools.partial(_hgrn_kernel, c=c, nb=nb, w=w),
        grid=(batch, steps),
        in_specs=[blk, blk, blk, blk, blk, _const_spec((1, w)), _const_spec((c, c)), _const_spec((w, w))],
        out_specs=[blk, pl.BlockSpec((None, w, w), lambda b, i: (b, 0, 0))],
        out_shape=[jax.ShapeDtypeStruct((t, w), BF16), jax.ShapeDtypeStruct((batch, w, w), F32)],
        scratch_shapes=[pltpu.VMEM((w, w), F32), pltpu.VMEM((nb, _hgrn_rows(c), w), BF16)],
        compiler_params=_params(("arbitrary", "arbitrary")),
        name="hgrn_prompt",
    )(bq, bk, bv, blf, bgs, lw["hgrn_g"], lw["tri_hgrn"], lw["g64"])


def _cum_kernel(lf_ref, tri_ref, o_ref, carry):
    i = pl.program_id(1)

    @pl.when(i == 0)
    def _():
        carry[...] = jnp.zeros_like(carry)

    lf = lf_ref[...]
    lf2 = jnp.concatenate([lf, lf], axis=0)
    cum = _dot3(lf2, tri_ref[...])[0:ROWS, :] + carry[:, 0:1]
    carry[...] = jnp.broadcast_to(cum[:, cum.shape[1] - 1:], carry.shape)
    pad = jnp.zeros((LANE - ROWS, cum.shape[1]), F32)
    o_ref[...] = jnp.concatenate([-cum, pad], axis=0).T


def _neg_cumsum(lft, tri, batch, tcs):
    rows, t = lft.shape
    npb = t // batch // tcs
    return pl.pallas_call(
        _cum_kernel,
        grid=(batch, npb),
        in_specs=[pl.BlockSpec((rows, tcs), lambda b, i: (0, b * npb + i)), _const_spec((tcs, tcs))],
        out_specs=pl.BlockSpec((tcs, LANE), lambda b, i: (b * npb + i, 0)),
        out_shape=jax.ShapeDtypeStruct((t, LANE), F32),
        scratch_shapes=[pltpu.VMEM((rows, LANE), F32)],
        compiler_params=_params(("arbitrary", "arbitrary")),
        name="fox_cumsum",
    )(lft, tri)


ATTN_QW = 256
ATTN_KC = 128
ATTN_BUFS = 4


def _attn_kernel(qi_ref, ki_ref, qt_ref, k_ref, vt_ref, bq_ref, bk_ref, lam_ref, gain_ref, bd_ref,
                 o_ref, qtm, m_scr, l_scr, acc_scr, bias_scr, s_scr, p_scr, *, r_rows, diff, slopes, t, w):
    j = pl.program_id(1)
    qi = qi_ref[j]
    ki = ki_ref[j]
    dh = w // N_HEADS
    width = w // r_rows
    qw, kc_len = min(ATTN_QW, t), min(ATTN_KC, t)
    key_iota = lax.broadcasted_iota(jnp.int32, (kc_len, qw), 0)

    @pl.when(ki == 0)
    def _():
        qt = qt_ref[...]
        owner = lax.broadcasted_iota(jnp.int32, (w, t), 0) // width
        for r in range(r_rows):
            qtm[r] = jnp.where(owner == r, qt, jnp.zeros_like(qt))
        m_scr[...] = jnp.full_like(m_scr, NEG)
        l_scr[...] = jnp.zeros_like(l_scr)
        acc_scr[...] = jnp.zeros_like(acc_scr)
        if diff:
            for h in range(N_HEADS):
                bias_scr[h] = (slopes[h] * LOG2E) * key_iota.astype(F32)

    def step(masked):
        if diff:
            tile_off = ((ki - qi) * t).astype(F32)
        else:
            for h in range(N_HEADS):
                col = (bk_ref[:, h:h + 1] - bq_ref[0:1, h:h + 1]) * LOG2E
                bias_scr[h] = jnp.broadcast_to(col, (t, LANE))

        def logits(g, r, q0):
            h = r * N_HEADS // r_rows
            n_keys = min(t, q0 + qw) if masked else t
            chunks = range(0, n_keys, kc_len)
            shifts = [slopes[h] * LOG2E * (tile_off + float(k0)) if diff else 0.0 for k0 in chunks]
            m_run = m_scr[r, :, q0:q0 + qw]
            m8 = jnp.full((SUBLANES, qw), NEG, F32)
            for k0, shift in zip(chunks, shifts):
                s = _dot(k_ref[k0:k0 + kc_len, :], qtm[r, :, q0:q0 + qw])
                if diff:
                    s = s + bias_scr[h]
                else:
                    s = s + jnp.concatenate([bias_scr[h, k0:k0 + kc_len, :]] * (qw // LANE), axis=1)
                if masked and k0 + kc_len - 1 > q0:
                    q_iota = lax.broadcasted_iota(jnp.int32, (kc_len, qw), 1)
                    s = jnp.where(key_iota + (k0 - q0) <= q_iota, s, NEG)
                s_scr[g % ATTN_BUFS, k0:k0 + kc_len, :] = s
                m8 = jnp.maximum(m8, jnp.max(s.reshape(kc_len // SUBLANES, SUBLANES, qw), axis=0) + shift)
            m_new = jnp.maximum(m_run, jnp.max(m8, axis=0, keepdims=True))
            return dict(buf=g % ATTN_BUFS, r=r, h=h, q0=q0, n_keys=n_keys, chunks=chunks, shifts=shifts,
                        m_new=m_new, alpha=jnp.exp2(m_run - m_new))

        def weigh(st):
            buf, r, h, q0, m_new, alpha = st["buf"], st["r"], st["h"], st["q0"], st["m_new"], st["alpha"]
            l8 = jnp.zeros((SUBLANES, qw), F32)
            for k0, shift in zip(st["chunks"], st["shifts"]):
                p = jnp.exp2(s_scr[buf, k0:k0 + kc_len, :] - (m_new - shift))
                l8 = l8 + jnp.sum(p.reshape(kc_len // SUBLANES, SUBLANES, qw), axis=0)
                p_scr[buf, k0:k0 + kc_len, :] = p.astype(BF16)
            l_add = jnp.sum(l8, axis=0, keepdims=True)
            pv = _dot(vt_ref[h * dh:(h + 1) * dh, 0:st["n_keys"]], p_scr[buf, 0:st["n_keys"], :])
            m_scr[r, :, q0:q0 + qw] = m_new
            l_scr[r, :, q0:q0 + qw] = alpha * l_scr[r, :, q0:q0 + qw] + l_add
            acc_scr[r, :, q0:q0 + qw] = alpha * acc_scr[r, :, q0:q0 + qw] + pv

        groups = [(r, q0) for r in range(r_rows) for q0 in range(0, t, qw)]
        ahead = ATTN_BUFS - 1
        pending = [logits(g, *groups[g]) for g in range(min(ahead, len(groups)))]
        for g in range(len(groups)):
            if g + ahead < len(groups):
                pending.append(logits(g + ahead, *groups[g + ahead]))
            weigh(pending.pop(0))

    @pl.when(ki < qi)
    def _():
        step(False)

    @pl.when(ki == qi)
    def _():
        step(True)
        heads = []
        for h in range(N_HEADS):
            if diff:
                o_h = (acc_scr[2 * h] * (1.0 / l_scr[2 * h])
                       - lam_ref[...] * (acc_scr[2 * h + 1] * (1.0 / l_scr[2 * h + 1])))
            else:
                o_h = acc_scr[h] * (1.0 / l_scr[h])
            heads.append(o_h)
        out = jnp.concatenate(heads, axis=0).T
        if diff:
            out = _group_norm(out, bd_ref, dh, gain_ref[...])
        o_ref[...] = out.astype(o_ref.dtype)


def _attn_prompt(qt, k, vt, bias, lam, gain, g64, batch, t_blk, r_rows, diff, slopes):
    w = qt.shape[1]
    t = k.shape[0]
    nq = t // batch // t_blk
    pairs = [(a, b) for a in range(nq) for b in range(a + 1)]
    qi = jnp.asarray([p[0] for p in pairs], jnp.int32)
    ki = jnp.asarray([p[1] for p in pairs], jnp.int32)
    oblk = pl.BlockSpec((t_blk, w), lambda b, j, qi, ki: (b * nq + qi[j], 0))
    kblk = pl.BlockSpec((t_blk, w), lambda b, j, qi, ki: (b * nq + ki[j], 0))
    qtblk = pl.BlockSpec((None, w, t_blk), lambda b, j, qi, ki: (b, 0, qi[j]))
    vtblk = pl.BlockSpec((None, w, t_blk), lambda b, j, qi, ki: (b, 0, ki[j]))
    if diff:
        bias = jnp.zeros((ROWS, LANE), F32)
        bias_specs = [_const_spec((ROWS, LANE))] * 2
        bias_scr = pltpu.VMEM((N_HEADS, min(ATTN_KC, t_blk), min(ATTN_QW, t_blk)), F32)
    else:
        bias_specs = [pl.BlockSpec((t_blk, LANE), lambda b, j, qi, ki: (b * nq + qi[j], 0)),
                      pl.BlockSpec((t_blk, LANE), lambda b, j, qi, ki: (b * nq + ki[j], 0))]
        bias_scr = pltpu.VMEM((N_HEADS, t_blk, LANE), F32)
    grid_spec = pltpu.PrefetchScalarGridSpec(
        num_scalar_prefetch=2,
        grid=(batch, len(pairs)),
        in_specs=[qtblk, kblk, vtblk, *bias_specs,
                  _const_spec((1, 1)), _const_spec((1, w)), _const_spec((w, w))],
        out_specs=oblk,
        scratch_shapes=[pltpu.VMEM((r_rows, w, t_blk), BF16),
                        pltpu.VMEM((r_rows, 1, t_blk), F32),
                        pltpu.VMEM((r_rows, 1, t_blk), F32),
                        pltpu.VMEM((r_rows, w // N_HEADS, t_blk), F32),
                        bias_scr,
                        pltpu.VMEM((ATTN_BUFS, t_blk, min(ATTN_QW, t_blk)), F32),
                        pltpu.VMEM((ATTN_BUFS, t_blk, min(ATTN_QW, t_blk)), BF16)])
    return pl.pallas_call(
        functools.partial(_attn_kernel, r_rows=r_rows, diff=diff, slopes=slopes, t=t_blk, w=w),
        grid_spec=grid_spec,
        out_shape=jax.ShapeDtypeStruct((t, w), BF16),
        compiler_params=_params(("arbitrary", "arbitrary")),
        name="diff_prompt" if diff else "fox_prompt",
    )(qi, ki, qt, k, vt, bias, bias, lam, gain, g64)


def _merge_kernel(x_ref, sh_ref, sc_ref, gt_ref, g1_ref, ya_ref, yb_ref, yc_ref, yd_ref,
                  wg_ref, wb_ref, wo_ref, o_ref, *, d, w):
    x = x_ref[...]
    h = _mod_norm(x, g1_ref[...], sc_ref[...], sh_ref[...]).astype(BF16)
    y_refs = (ya_ref, yb_ref, yc_ref, yd_ref)

    def branch(j):
        return (_dot(h, wg_ref[:, j * d:(j + 1) * d]),
                _dot(y_refs[j][...].astype(BF16), wb_ref[j * w:(j + 1) * w, :]))

    merged = jnp.zeros(x.shape, F32)
    nxt = branch(0)
    for j in range(len(y_refs)):
        gate_logits, proj = nxt
        if j + 1 < len(y_refs):
            nxt = branch(j + 1)
        merged = merged + _sigmoid(gate_logits) * proj
    o_ref[...] = x + gt_ref[...] * _dot(merged.astype(BF16), wo_ref[...])


def _merge(x, mod, ys, lw, tm, tiles_per_mod):
    t, d = x.shape
    w = d // 4
    r = mod.shape[1]
    row = lambda i: (i, 0)
    modspec = lambda col: pl.BlockSpec((None, r, d), lambda i: (i // tiles_per_mod, 0, col))
    yspec = pl.BlockSpec((tm, w), row)
    return pl.pallas_call(
        functools.partial(_merge_kernel, d=d, w=w),
        grid=(t // tm,),
        in_specs=[pl.BlockSpec((tm, d), row), modspec(0), modspec(1), modspec(2), _const_spec((1, d)),
                  yspec, yspec, yspec, yspec,
                  _const_spec((d, 4 * d)), _const_spec((d, d)), _const_spec((d, d))],
        out_specs=pl.BlockSpec((tm, d), row),
        out_shape=jax.ShapeDtypeStruct((t, d), F32),
        compiler_params=_params(("arbitrary",)),
        name="merge_out",
    )(x, mod, mod, mod, lw["g1"], *ys, lw["w_gates"], lw["w_branch"], lw["w_out"])


def _ffn_kernel(x_ref, sh_ref, sc_ref, gt_ref, g2_ref, w1_ref, w2_ref, o_ref, *, d, n_chunks):
    x = x_ref[...]
    h = _mod_norm(x, g2_ref[...], sc_ref[...], sh_ref[...]).astype(BF16)
    def up(cidx):
        return _dot(h, w1_ref[:, cidx * d:(cidx + 1) * d])

    acc = jnp.zeros(x.shape, F32)
    nxt = up(0)
    for cidx in range(n_chunks):
        u = jnp.maximum(nxt, 0.0)
        if cidx + 1 < n_chunks:
            nxt = up(cidx + 1)
        acc = acc + _dot((u * u).astype(BF16), w2_ref[cidx * d:(cidx + 1) * d, :])
    o_ref[...] = x + gt_ref[...] * acc


def _ffn(x, mod, lw, tm, tiles_per_mod):
    t, d = x.shape
    dff = lw["w_ff1"].shape[1]
    r = mod.shape[1]
    row = lambda i: (i, 0)
    modspec = lambda col: pl.BlockSpec((None, r, d), lambda i: (i // tiles_per_mod, 0, col))
    return pl.pallas_call(
        functools.partial(_ffn_kernel, d=d, n_chunks=dff // d),
        grid=(t // tm,),
        in_specs=[pl.BlockSpec((tm, d), row), modspec(3), modspec(4), modspec(5), _const_spec((1, d)),
                  _const_spec((d, dff)), _const_spec((dff, d))],
        out_specs=pl.BlockSpec((tm, d), row),
        out_shape=jax.ShapeDtypeStruct((t, d), F32),
        compiler_params=_params(("arbitrary",)),
        name="ffn",
    )(x, mod, mod, mod, lw["g2"], lw["w_ff1"], lw["w_ff2"])


def _conv_step_kernel(state_ref, glu_ref, w_ref, b_ref, g_ref, beta_ref, o_ref):
    n = CONV_W - 1
    acc = glu_ref[...] * w_ref[n:n + 1, :] + b_ref[...]
    for j in range(n):
        acc = acc + state_ref[j] * w_ref[j:j + 1, :]
    o_ref[...] = _ln_silu(acc, g_ref[...], beta_ref[...])


def _conv_step(state_t, glu, lw):
    db, w = glu.shape
    return pl.pallas_call(
        _conv_step_kernel,
        out_shape=jax.ShapeDtypeStruct((db, w), F32),
        compiler_params=_params(None),
        name="conv_step",
    )(state_t, glu, lw["conv_w"], lw["conv_b"], lw["conv_ln_g"], lw["conv_ln_b"])


def _hgrn_step_kernel(s0_ref, qc_ref, kc_ref, lfc_ref, vr_ref, gsr_ref, gain_ref, y_ref, s_ref):
    s_new = jnp.exp(lfc_ref[...]) * s0_ref[...] + kc_ref[...] * vr_ref[...]
    s_ref[...] = s_new
    o = jnp.sum(qc_ref[...] * s_new, axis=2, keepdims=True)
    ms = jnp.mean(o * o, axis=-1, keepdims=True)
    y_ref[...] = o * lax.rsqrt(ms + EPS) * gain_ref[...] * gsr_ref[...]


def _hgrn_step(s0, bq, bk, bv, blf, bgs, gain):
    db, nh, dk, dv = s0.shape
    col = lambda a: a.reshape(db, nh, dk, 1)
    rowv = lambda a: a.reshape(db, nh, 1, dv)
    y, s = pl.pallas_call(
        _hgrn_step_kernel,
        out_shape=[jax.ShapeDtypeStruct((db, nh, 1, dv), F32), jax.ShapeDtypeStruct(s0.shape, F32)],
        compiler_params=_params(None),
        name="hgrn_step",
    )(s0, col(bq), col(bk), col(blf), rowv(bv), rowv(bgs), gain.reshape(1, 1, 1, dv))
    return y.reshape(db, nh * dv), s


def _suffix_kernel(lf_ref, m_ref, o_ref):
    o_ref[...] = _dot3(lf_ref[...], m_ref[...])


def _suffix_matrix(page):
    src = np.arange(page)[:, None]
    dst = np.arange(2 * page)[None, :]
    return jnp.asarray((dst >= page) | (src > dst), BF16)


def _suffix_table(logf_rows, page):
    n = logf_rows.shape[0]
    tr = math.gcd(n, 1024)
    out = pl.pallas_call(
        _suffix_kernel,
        grid=(n // tr,),
        in_specs=[pl.BlockSpec((tr, page), lambda i: (i, 0)), _const_spec((page, 2 * page))],
        out_specs=pl.BlockSpec((tr, 2 * page), lambda i: (i, 0)),
        out_shape=jax.ShapeDtypeStruct((n, 2 * page), F32),
        compiler_params=_params(("arbitrary",)),
        name="fox_suffix_table",
    )(logf_rows, _suffix_matrix(page))
    return out.reshape(n // ROWS, ROWS, 2 * page)


def _decode_kernel(pt_ref, qm_ref, kn_ref, vn_ref, cnew_ref, slope_ref, lam_ref, gain_ref,
                   kc_ref, vc_ref, wt_ref, o_ref,
                   kbuf, vbuf, wbuf, sem, m_scr, l_scr, acc_scr, carry, p_scr, alpha_scr,
                   *, fox, scale, g_pages, n_groups, n_seq, page, nbuf, page_base, w):
    n_it = n_seq * n_groups
    past = n_groups * g_pages * page
    n_live = N_HEADS if fox else 2 * N_HEADS
    dh = w // N_HEADS
    lane0 = lax.broadcasted_iota(jnp.int32, (dh, page), 1) == 0

    def copies(it, slot):
        b = it // n_groups
        c = n_groups - 1 - it % n_groups
        out = []
        for g in range(g_pages):
            pg = page_base + pt_ref[b, c * g_pages + g]
            out.append(pltpu.make_async_copy(kc_ref.at[pg], kbuf.at[slot, g], sem.at[0, slot]))
            out.append(pltpu.make_async_copy(vc_ref.at[pg], vbuf.at[slot, g], sem.at[1, slot]))
            if fox:
                out.append(pltpu.make_async_copy(wt_ref.at[pg], wbuf.at[slot, g], sem.at[2, slot]))
        return out

    ahead = nbuf - 2

    def score(it):
        slot = it % nbuf
        b = it // n_groups
        c = n_groups - 1 - it % n_groups
        first = c == n_groups - 1
        qm = qm_ref[b]
        s = jnp.concatenate([_dot(qm, kbuf[slot, g]) for g in range(g_pages)], axis=1) * scale
        if fox:
            parts = [None] * g_pages
            run = jnp.where(first, 0.0, carry[...])
            for g in reversed(range(g_pages)):
                parts[g] = wbuf[slot, g, :, 0:page] + run
                run = run + wbuf[slot, g, :, page:2 * page]
            carry[...] = run
            s = s + jnp.concatenate(parts, axis=1) + cnew_ref[b]
        else:
            pos = (c * (g_pages * page) + lax.broadcasted_iota(jnp.int32, (1, g_pages * page), 1)).astype(F32)
            s = s + slope_ref[...] * (pos - float(past))
        m_prev = jnp.where(first, jnp.sum(qm * kn_ref[b], axis=-1, keepdims=True) * scale, m_scr[...])
        l_prev = jnp.where(first, 1.0, l_scr[...])
        m_new = jnp.maximum(m_prev, jnp.max(s, axis=-1, keepdims=True))
        alpha = jnp.exp(m_prev - m_new)
        p = jnp.exp(s - m_new)
        l_scr[...] = alpha * l_prev + jnp.sum(p, axis=-1, keepdims=True)
        m_scr[...] = m_new
        p_scr[...] = p
        alpha_scr[...] = alpha

    def weigh(jt):
        slot = jt % nbuf
        b = jt // n_groups
        c = n_groups - 1 - jt % n_groups
        first = c == n_groups - 1
        l_now = l_scr[...]
        for r in range(n_live):
            h = r * N_HEADS // n_live
            rows = slice(h * dh, (h + 1) * dh)
            start = jnp.where(lane0, vn_ref[b, rows, :], 0.0)
            acc = jnp.where(first, start, acc_scr[r]) * alpha_scr[r:r + 1, :]
            for g in range(g_pages):
                acc = acc + vbuf[slot, g, rows, :] * p_scr[r:r + 1, g * page:(g + 1) * page]
            acc_scr[r] = acc
        return l_now, b, c == 0

    def finish(l_now, b):
        cols = [jnp.sum(acc_scr[r], axis=1, keepdims=True) / l_now[r:r + 1, :] for r in range(n_live)]
        heads = []
        for h in range(N_HEADS):
            if fox:
                o_h = cols[h]
            else:
                o_h = cols[2 * h] - lam_ref[...] * cols[2 * h + 1]
                ms = jnp.mean(o_h * o_h, axis=0, keepdims=True)
                o_h = o_h * lax.rsqrt(ms + EPS) * gain_ref[h * dh:(h + 1) * dh, :]
            heads.append(o_h)
        o_ref[b] = jnp.concatenate(heads, axis=0)

    for it0 in range(min(ahead, n_it)):
        for cp in copies(it0, it0 % nbuf):
            cp.start()

    for ref in (p_scr, alpha_scr, acc_scr, carry, m_scr):
        ref[...] = jnp.zeros_like(ref)
    l_scr[...] = jnp.ones_like(l_scr)

    def body(it, _):
        @pl.when(it + ahead < n_it)
        def _():
            for cp in copies(it + ahead, (it + ahead) % nbuf):
                cp.start()

        @pl.when(it < n_it)
        def _():
            for cp in copies(it, it % nbuf):
                cp.wait()

        l_now, b, last = weigh(jnp.maximum(it - 1, 0))
        score(jnp.minimum(it, n_it - 1))

        @pl.when(jnp.logical_and(last, it > 0))
        def _():
            finish(l_now, b)

        return 0

    lax.fori_loop(0, n_it + 1, body, 0)


def _decode_attn(page_table, qm, k_new, v_new, cnew, slopes, lam, gain_col, kt_cache, vt_cache, w_table,
                 layer, n_pool, fox, scale):
    db, n_pages = page_table.shape
    w, page = kt_cache.shape[1], kt_cache.shape[2]
    g_pages = math.gcd(n_pages, 8)
    nbuf = 7
    n_groups = n_pages // g_pages
    full3 = lambda shape: pl.BlockSpec(shape, lambda i, pt: (0, 0, 0))
    full2 = lambda shape: pl.BlockSpec(shape, lambda i, pt: (0, 0))
    any_spec = pl.BlockSpec(memory_space=pl.ANY)
    grid_spec = pltpu.PrefetchScalarGridSpec(
        num_scalar_prefetch=1,
        grid=(1,),
        in_specs=[full3((db, ROWS, w)), full3((db, 1, w)), full3((db, w, 1)), full3((db, ROWS, 1)),
                  full2((ROWS, 1)), full2((1, 1)), full2((w, 1)),
                  any_spec, any_spec, any_spec],
        out_specs=full3((db, w, 1)),
        scratch_shapes=[pltpu.VMEM((nbuf, g_pages, w, page), F32),
                        pltpu.VMEM((nbuf, g_pages, w, page), F32),
                        pltpu.VMEM((nbuf, g_pages, ROWS, 2 * page), F32),
                        pltpu.SemaphoreType.DMA((3, nbuf)),
                        pltpu.VMEM((ROWS, 1), F32), pltpu.VMEM((ROWS, 1), F32),
                        pltpu.VMEM((ROWS, w // N_HEADS, page), F32), pltpu.VMEM((ROWS, page), F32),
                        pltpu.VMEM((ROWS, g_pages * page), F32), pltpu.VMEM((ROWS, 1), F32)])
    out = pl.pallas_call(
        functools.partial(_decode_kernel, fox=fox, scale=scale, g_pages=g_pages,
                          n_groups=n_groups, n_seq=db, page=page, nbuf=nbuf, page_base=layer * n_pool, w=w),
        grid_spec=grid_spec,
        out_shape=jax.ShapeDtypeStruct((db, w, 1), F32),
        compiler_params=_params(("arbitrary",)),
        name="fox_decode" if fox else "diff_decode",
    )(page_table, qm, k_new.reshape(db, 1, w), v_new.reshape(db, w, 1), cnew, slopes, lam, gain_col,
      kt_cache, vt_cache, w_table)
    return out.reshape(db, w)


def _block_diag_ones(w, n):
    idx = np.arange(w) // n
    return jnp.asarray(idx[:, None] == idx[None, :], BF16)


def _row_mask(w, r_rows):
    lane = np.arange(w) // (w // r_rows)
    return jnp.asarray(np.arange(ROWS)[:, None] == lane[None, :], F32)


def _alibi_slopes(n):
    return jnp.asarray(2.0 ** (-8.0 * np.arange(1, n + 1) / n), F32)


def _layer_weights(l, p, lbs, d):
    w = d // 4
    o_f = 9 * w
    o_d = o_f + N_HEADS
    w_in_t = p["w_in"].transpose(2, 0, 1)
    wt_abc = w_in_t[:o_f, l]
    wt_f = w_in_t[o_f:o_d, l]
    wt_rest = w_in_t[o_d:, l]
    lb = lbs[l]
    lbf = jnp.maximum(lb, LB_FLOOR)
    tile = lambda a, n: jnp.tile(a, n).reshape(1, w)
    conv_w = jnp.concatenate([p["conv_w"][l], jnp.zeros((CONV_HALO - CONV_W, w), F32)], axis=0)
    lam_init = 0.8 - 0.6 * math.exp(-0.3 * l)
    lam = (jnp.exp(jnp.sum(p["lam_q1"][l] * p["lam_k1"][l])) - jnp.exp(jnp.sum(p["lam_q2"][l] * p["lam_k2"][l]))
           + lam_init).astype(F32)
    return dict(
        g1=p["norm1_g"][l].reshape(1, d), g2=p["norm2_g"][l].reshape(1, d),
        w_abc=_weight_prep(wt_abc, 0, o_f), w_d=_weight_prep(wt_rest, 0, 3 * w),
        w_gates=_weight_prep(wt_rest, 3 * w, 4 * d),
        w_ft=jnp.pad(wt_f, ((0, ROWS - N_HEADS), (0, 0))).astype(BF16),
        bf_col=jnp.pad(p["b_fox_f"][l], (0, ROWS - N_HEADS)).reshape(ROWS, 1),
        one_minus_lb=(1.0 - lb).reshape(1, w), dlb=(lbf - lb).reshape(1, w),
        fqn=tile(p["fox_qn_g"][l], N_HEADS), fkn=tile(p["fox_kn_g"][l], N_HEADS),
        dqn=tile(p["diff_qn_g"][l], 2 * N_HEADS), dkn=tile(p["diff_kn_g"][l], 2 * N_HEADS),
        hgrn_g=tile(p["hgrn_norm_g"][l], N_HEADS), hgrn_g1=p["hgrn_norm_g"][l],
        diff_g=tile(p["diff_norm_g"][l], N_HEADS) * (1.0 - lam_init),
        conv_w=conv_w, conv_b=p["conv_b"][l].reshape(1, w),
        conv_ln_g=p["conv_ln_g"][l].reshape(1, w), conv_ln_b=p["conv_ln_b"][l].reshape(1, w),
        lam=lam.reshape(1, 1),
        w_branch=p["w_branch"][l].reshape(d, d).astype(BF16),
        w_out=p["w_out"][l].astype(BF16),
        w_ff1=p["w_ff1"][l].astype(BF16), w_ff2=p["w_ff2"][l].astype(BF16),
        ones_w=jnp.ones((1, w), F32),
    )


def _hgrn_state_from_transposed(st, w):
    b = st.shape[0]
    dh = w // N_HEADS
    st5 = st.reshape(b, N_HEADS, dh, N_HEADS, dh)
    return jnp.stack([st5[:, h, :, h, :] for h in range(N_HEADS)], axis=1).transpose(0, 1, 3, 2)


def kernel(x_prompt, x_sample, cache_fox_k, cache_fox_v, cache_fox_logf, cache_diff_k, cache_diff_v,
           state_conv, state_hgrn, page_table, c_prompt, c_sample, w_ada, b_ada, norm1_g, norm2_g, w_in,
           b_fox_f, lb_logits, hgrn_norm_g, conv_w, conv_b, conv_ln_g, conv_ln_b, fox_qn_g, fox_kn_g,
           diff_qn_g, diff_kn_g, lam_q1, lam_k1, lam_q2, lam_k2, diff_norm_g, w_branch, w_out, w_ff1, w_ff2):
    p = dict(norm1_g=norm1_g, norm2_g=norm2_g, w_in=w_in, b_fox_f=b_fox_f, hgrn_norm_g=hgrn_norm_g,
             conv_w=conv_w, conv_b=conv_b, conv_ln_g=conv_ln_g, conv_ln_b=conv_ln_b, fox_qn_g=fox_qn_g,
             fox_kn_g=fox_kn_g, diff_qn_g=diff_qn_g, diff_kn_g=diff_kn_g, lam_q1=lam_q1, lam_k1=lam_k1,
             lam_q2=lam_q2, lam_k2=lam_k2, diff_norm_g=diff_norm_g, w_branch=w_branch, w_out=w_out,
             w_ff1=w_ff1, w_ff2=w_ff2)
    batch, seq, d = x_prompt.shape
    db = x_sample.shape[0]
    assert x_sample.shape[1] == 1, "the sample group advances one token per step"
    depth = w_in.shape[0]
    w = d // 4
    dh = w // N_HEADS
    n_pool, page = cache_fox_k.shape[1], cache_fox_k.shape[2]
    n_pages = page_table.shape[1]

    lbs = jax.nn.softmax(lb_logits.astype(F32), axis=0)
    lbs = jnp.cumsum(lbs, axis=0) - lbs[0]
    lws = [_layer_weights(l, p, lbs, d) for l in range(depth)]
    g64 = _block_diag_ones(w, dh)
    g32 = _block_diag_ones(w, dh // 2)
    for lw in lws:
        lw["g64"], lw["g32"] = g64, g32

    rows = -(-(batch + db) // 8) * 8
    c_all = jnp.concatenate([c_prompt, c_sample, jnp.zeros((rows - batch - db, d), F32)], axis=0)
    mod = _ada(c_all, w_ada, b_ada)
    mod_p = mod[:, :batch].reshape(depth, batch, 1, 6 * d)
    mod_s = mod[:, batch:batch + db].reshape(depth, 1, db, 6 * d)

    t = batch * seq
    tm = math.gcd(seq, 512)
    t_attn = math.gcd(seq, 1024)
    tcs = math.gcd(seq, 512)
    tri_cum = jnp.asarray(np.triu(np.ones((tcs, tcs))), BF16)
    tri_hgrn = jnp.asarray(np.tril(np.ones((HGRN_CHUNK, HGRN_CHUNK))), BF16)
    slope_vals = tuple(float(2.0 ** (-8.0 * (h + 1) / N_HEADS)) for h in range(N_HEADS))
    slopes = jnp.asarray(slope_vals, F32)

    def cache_layout(a):
        return a.reshape(a.shape[0], a.shape[1], N_HEADS, dh, a.shape[3]).transpose(0, 1, 4, 2, 3)

    x = x_prompt.reshape(t, d)
    kv_p = [jnp.zeros((depth, batch, w, seq), F32) for _ in range(4)]
    p_conv, p_hgrn, p_logf = [], [], []
    for l, lw in enumerate(lws):
        lw["tri_hgrn"] = tri_hgrn
        outs = _inproj(x, mod_p[l], lw, tm, batch, l, depth, kv_p)
        glu, bq, bk, bv, blf, bgs, ckb, dkb = outs[:8]
        kv_p = outs[8:12]
        cqtb, cvtb, dqtb, dvtb, clft = outs[12:]
        ya = _conv_prompt(glu, lw, batch, math.gcd(seq, 256))
        yb, st = _hgrn_prompt(bq, bk, bv, blf, bgs, lw, batch)
        fox_bias = _neg_cumsum(clft, tri_cum, batch, tcs)
        yc = _attn_prompt(cqtb, ckb, cvtb, fox_bias, lw["lam"], lw["ones_w"], g64, batch, t_attn,
                          N_HEADS, False, slope_vals)
        yd = _attn_prompt(dqtb, dkb, dvtb, None, lw["lam"], lw["diff_g"], g64, batch, t_attn,
                          2 * N_HEADS, True, slope_vals)
        x = _merge(x, mod_p[l], (ya, yb, yc, yd), lw, tm, seq // tm)
        x = _ffn(x, mod_p[l], lw, tm, seq // tm)
        p_conv.append(glu.reshape(batch, seq, w)[:, seq - (CONV_W - 1):])
        p_hgrn.append(_hgrn_state_from_transposed(st, w))
        p_logf.append(clft[:N_HEADS].reshape(N_HEADS, batch, seq).transpose(1, 2, 0))
    y_prompt = x.reshape(batch, seq, d)
    p_out = [jnp.stack(p_conv), jnp.stack(p_hgrn), cache_layout(kv_p[0]), cache_layout(kv_p[1]),
             jnp.stack(p_logf), cache_layout(kv_p[2]), cache_layout(kv_p[3])]

    pool_t = lambda a: a.transpose(0, 1, 3, 4, 2).reshape(depth * n_pool, w, page)
    fox_kt, fox_vt = pool_t(cache_fox_k), pool_t(cache_fox_v)
    diff_kt, diff_vt = pool_t(cache_diff_k), pool_t(cache_diff_v)
    logf_rows = jnp.pad(cache_fox_logf.transpose(0, 1, 3, 2), ((0, 0), (0, 0), (0, ROWS - N_HEADS), (0, 0)))
    w_table = _suffix_table(logf_rows.reshape(depth * n_pool * ROWS, page), page)
    mask4 = _row_mask(w, N_HEADS)
    mask8 = _row_mask(w, 2 * N_HEADS)
    slope_col = jnp.repeat(slopes, 2).reshape(ROWS, 1)
    zero_col = jnp.zeros((db, ROWS, 1), F32)
    head_rows = jnp.arange(ROWS)[:, None] < N_HEADS
    conv_t = state_conv.transpose(0, 2, 1, 3)

    x = x_sample.reshape(db, d)
    kv_s = [jnp.zeros((depth, 1, w, db), F32) for _ in range(4)]
    s_conv, s_hgrn, s_logf = [], [], []
    for l, lw in enumerate(lws):
        outs = _inproj(x, mod_s[l], lw, db, 1, l, depth, kv_s)
        glu, bq, bk, bv, blf, bgs = outs[:6]
        kv_s = outs[8:12]
        cqb, dqb = outs[12][0].T, outs[14][0].T
        clft = outs[16]
        ck, cv, dk, dv = [a[l, 0].T for a in kv_s]
        ya = _conv_step(conv_t[l], glu, lw)
        yb, s_new = _hgrn_step(state_hgrn[l], bq, bk, bv, blf, bgs, lw["hgrn_g1"])
        cnew = jnp.where(head_rows, clft, 0.0).T.reshape(db, ROWS, 1)
        qm_fox = (cqb.astype(F32) * (1.0 / LOG2E))[:, None, :] * mask4[None]
        yc = _decode_attn(page_table, qm_fox, ck, cv, cnew, slope_col, lw["lam"], lw["ones_w"].reshape(w, 1),
                          fox_kt, fox_vt, w_table, l, n_pool, True, 1.0)
        qm_diff = (dqb.astype(F32) * (1.0 / LOG2E))[:, None, :] * mask8[None]
        yd = _decode_attn(page_table, qm_diff, dk, dv, zero_col, slope_col, lw["lam"], lw["diff_g"].reshape(w, 1),
                          diff_kt, diff_vt, w_table, l, n_pool, False, 1.0)
        x = _merge(x, mod_s[l], (ya, yb, yc, yd), lw, db, 1)
        x = _ffn(x, mod_s[l], lw, db, 1)
        s_conv.append(jnp.concatenate([conv_t[l][1:], glu[None]], axis=0).transpose(1, 0, 2))
        s_hgrn.append(s_new)
        s_logf.append(clft[:N_HEADS].T.reshape(db, 1, N_HEADS))
    y_sample = x.reshape(db, 1, d)
    sample_layout = lambda a: cache_layout(a).transpose(0, 2, 1, 3, 4)
    s_out = [jnp.stack(s_conv), jnp.stack(s_hgrn), sample_layout(kv_s[0]), sample_layout(kv_s[1]),
             jnp.stack(s_logf), sample_layout(kv_s[2]), sample_layout(kv_s[3])]
    return (y_prompt, y_sample, *p_out, *s_out)
```
